```python
import jax, jax.numpy as jnp
from jax import lax
import numpy as np

D_MODEL = 4096
BATCH = 2
SEQ = 8192
DEPTH = 2
DEC_BATCH = 8
DEC_SEQ = 16
PAST_LEN = 2048

CHUNK = 64
D_MIX = D_MODEL
D_A = D_MIX // 2
D_B = D_MIX - D_A
HEAD_DIM = 128
H_A = D_A // HEAD_DIM
LRU_BLOCK = 128
H_B = D_B // LRU_BLOCK
CONV_W = 4
LRU_C = 8.0
D_FF = 4 * D_MODEL
N_IN = 4 * D_A + 2 * H_A + 2 * D_B
SPLITS = (3 * D_A, 4 * D_A, 4 * D_A + H_A, 4 * D_A + 2 * H_A, 4 * D_A + 2 * H_A + D_B)
EPS = 1e-6

kernel_name = "hymba_gdn_rglru_stream_step"


def _rmsnorm(x, w):
    xf = x.astype(jnp.float32)
    y = xf * lax.rsqrt(jnp.mean(xf * xf, axis=-1, keepdims=True) + EPS)
    return (y * w.astype(jnp.float32)).astype(x.dtype)


def _l2norm(x):
    xf = x.astype(jnp.float32)
    return xf * lax.rsqrt(jnp.sum(xf * xf, axis=-1, keepdims=True) + EPS)


def _causal_conv(x, prev, w):
    L = x.shape[1]
    xp = jnp.concatenate([prev.astype(x.dtype), x], axis=1)
    y = xp[:, 0:L] * w[0]
    for j in range(1, CONV_W):
        y = y + xp[:, j:j + L] * w[j]
    return y, xp[:, L:]


def _gated_delta_rule(q, k, v, g, beta, s0):
    B, L, H, K = q.shape
    V = v.shape[-1]
    pad = (-L) % CHUNK
    n = (L + pad) // CHUNK

    def blocks(t):
        t = jnp.pad(t, [(0, 0), (0, pad)] + [(0, 0)] * (t.ndim - 2))
        t = jnp.swapaxes(t, 1, 2)
        return t.reshape((B, H, n, CHUNK) + t.shape[3:])

    qc, kc, vc, gc, bc = blocks(q), blocks(k), blocks(v), blocks(g), blocks(beta)
    G = jnp.cumsum(gc, axis=-1)
    idx = jnp.arange(CHUNK)
    causal = idx[:, None] >= idx[None, :]
    strict = idx[:, None] > idx[None, :]
    diff = G[..., :, None] - G[..., None, :]
    decay = jnp.where(causal, jnp.exp(jnp.where(causal, diff, 0.0)), 0.0)
    kb = kc * bc[..., None]
    a_mat = jnp.where(strict, jnp.einsum('bhnik,bhnjk->bhnij', kb, kc) * decay, 0.0)
    rhs = jnp.concatenate([vc * bc[..., None], kb * jnp.exp(G)[..., None]], axis=-1)
    lhs = jnp.eye(CHUNK, dtype=jnp.float32) + a_mat
    sol = lax.linalg.triangular_solve(lhs, rhs, left_side=True, lower=True)
    u, w = sol[..., :V], sol[..., V:]
    qk = jnp.einsum('bhnik,bhnjk->bhnij', qc, kc) * decay
    qg = qc * jnp.exp(G)[..., None]
    kd = kc * jnp.exp(G[..., -1:] - G)[..., None]
    glast = jnp.exp(G[..., -1])

    def step(S, xs):
        u_i, w_i, qk_i, qg_i, kd_i, gl_i = xs
        v_new = u_i - jnp.einsum('bhck,bhkv->bhcv', w_i, S)
        o_i = jnp.einsum('bhck,bhkv->bhcv', qg_i, S) + jnp.einsum('bhij,bhjv->bhiv', qk_i, v_new)
        S = S * gl_i[..., None, None] + jnp.einsum('bhck,bhcv->bhkv', kd_i, v_new)
        return S, o_i

    xs = (jnp.moveaxis(u, 2, 0), jnp.moveaxis(w, 2, 0), jnp.moveaxis(qk, 2, 0),
          jnp.moveaxis(qg, 2, 0), jnp.moveaxis(kd, 2, 0), jnp.moveaxis(glast, 2, 0))
    s_final, o = lax.scan(step, s0, xs)
    o = jnp.moveaxis(o, 0, 2).reshape(B, H, n * CHUNK, V)[:, :, :L]
    return jnp.swapaxes(o, 1, 2), s_final


def _lru_scan(a, b, h0):
    b = b.at[:, 0].add(a[:, 0] * h0)

    def combine(lhs, rhs):
        a_l, b_l = lhs
        a_r, b_r = rhs
        return a_r * a_l, a_r * b_l + b_r

    _, h = lax.associative_scan(combine, (a, b), axis=1)
    return h, h[:, -1]


def _layer(x, conv_d, s_d, conv_l, h_l, p):
    B, L, _ = x.shape
    f32 = jnp.float32
    xn = _rmsnorm(x, p['w_norm_mix'])
    proj = xn @ p['w_in']
    qkv, z, b_logit, a_logit, xl, yl = jnp.split(proj, SPLITS, axis=-1)

    qkv_c, conv_d_new = _causal_conv(qkv, conv_d, p['w_conv_delta'])
    qkv_c = jax.nn.silu(qkv_c).reshape(B, L, 3, H_A, HEAD_DIM)
    q = _l2norm(qkv_c[:, :, 0]) * (HEAD_DIM ** -0.5)
    k = _l2norm(qkv_c[:, :, 1])
    v = qkv_c[:, :, 2].astype(f32)
    beta = jax.nn.sigmoid(b_logit.astype(f32))
    g = -jnp.exp(p['a_log'].astype(f32)) * jax.nn.softplus(a_logit.astype(f32) + p['dt_bias'].astype(f32))
    o, s_d_new = _gated_delta_rule(q, k, v, g, beta, s_d.astype(f32))
    o = _rmsnorm(o, p['w_norm_delta']) * jax.nn.silu(z.astype(f32).reshape(B, L, H_A, HEAD_DIM))
    delta_out = o.reshape(B, L, D_A).astype(x.dtype)

    xc, conv_l_new = _causal_conv(xl, conv_l, p['w_conv_lru'])
    xc = (xc + p['b_conv_lru']).astype(f32).reshape(B, L, H_B, LRU_BLOCK)
    gate_r = jax.nn.sigmoid(jnp.einsum('blhi,hij->blhj', xc, p['w_gate_a'].astype(f32))
                            + p['b_gate_a'].astype(f32).reshape(H_B, LRU_BLOCK))
    gate_i = jax.nn.sigmoid(jnp.einsum('blhi,hij->blhj', xc, p['w_gate_x'].astype(f32))
                            + p['b_gate_x'].astype(f32).reshape(H_B, LRU_BLOCK))
    log_a = -LRU_C * gate_r * jax.nn.softplus(-p['lam'].astype(f32).reshape(H_B, LRU_BLOCK))
    a = jnp.exp(log_a)
    b_in = jnp.sqrt(-jnp.expm1(2.0 * log_a)) * gate_i * xc
    h, h_last = _lru_scan(a.reshape(B, L, D_B), b_in.reshape(B, L, D_B), h_l.astype(f32))
    lru_out = _rmsnorm(jax.nn.gelu(yl.astype(f32)) * h, p['w_norm_lru']).astype(x.dtype)

    x = x + jnp.concatenate([delta_out, lru_out], axis=-1) @ p['w_out']
    hid = jax.nn.relu(_rmsnorm(x, p['w_norm_mlp']) @ p['w_mlp_up'])
    x = x + (hid * hid) @ p['w_mlp_down']
    return x, s_d_new, conv_d_new, h_last, conv_l_new


def _trunk(x, s_d, conv_d, h_l, conv_l, layers, w_norm_final):
    sds, cds, hls, cls = [], [], [], []
    for i in range(DEPTH):
        x, sd, cd, hl, cl = _layer(x, conv_d[i], s_d[i], conv_l[i], h_l[i], layers[i])
        sds.append(sd.astype(x.dtype))
        cds.append(cd.astype(x.dtype))
        hls.append(hl.astype(x.dtype))
        cls.append(cl.astype(x.dtype))
    y = _rmsnorm(x, w_norm_final)
    return y, jnp.stack(sds), jnp.stack(cds), jnp.stack(hls), jnp.stack(cls)


def setup_inputs(seed: int = 0) -> dict:
    key = jax.random.key(seed)
    ks = jax.random.split(key, 26)
    f32 = jnp.float32
    nrm = lambda k, shape, s: jax.random.normal(k, shape, f32) * s
    x_prompt = jax.random.normal(ks[0], (BATCH, SEQ, D_MODEL), f32)
    x_sample = jax.random.normal(ks[1], (DEC_BATCH, DEC_SEQ, D_MODEL), f32)
    state_delta = nrm(ks[2], (DEPTH, DEC_BATCH, H_A, HEAD_DIM, HEAD_DIM), 0.1)
    state_conv_delta = jax.random.normal(ks[3], (DEPTH, DEC_BATCH, CONV_W - 1, 3 * D_A), f32)
    state_lru = nrm(ks[4], (DEPTH, DEC_BATCH, D_B), 0.5)
    state_conv_lru = jax.random.normal(ks[5], (DEPTH, DEC_BATCH, CONV_W - 1, D_B), f32)
    w_norm_mix = 1.0 + nrm(ks[6], (DEPTH, D_MODEL), 0.02)
    w_in = nrm(ks[7], (DEPTH, D_MODEL, N_IN), D_MODEL ** -0.5)
    w_conv_delta = nrm(ks[8], (DEPTH, CONV_W, 3 * D_A), CONV_W ** -0.5)
    a_log = jnp.log(jax.random.uniform(ks[9], (DEPTH, H_A), f32, 1.0, 16.0))
    dt = jnp.exp(jax.random.uniform(ks[10], (DEPTH, H_A), f32, np.log(1e-3), np.log(1e-1)))
    dt_bias = dt + jnp.log(-jnp.expm1(-dt))
    w_norm_delta = 1.0 + nrm(ks[11], (DEPTH, HEAD_DIM), 0.02)
    w_conv_lru = nrm(ks[12], (DEPTH, CONV_W, D_B), CONV_W ** -0.5)
    b_conv_lru = nrm(ks[13], (DEPTH, D_B), 0.02)
    w_gate_a = nrm(ks[14], (DEPTH, H_B, LRU_BLOCK, LRU_BLOCK), LRU_BLOCK ** -0.5)
    b_gate_a = nrm(ks[15], (DEPTH, D_B), 0.02)
    w_gate_x = nrm(ks[16], (DEPTH, H_B, LRU_BLOCK, LRU_BLOCK), LRU_BLOCK ** -0.5)
    b_gate_x = nrm(ks[17], (DEPTH, D_B), 0.02)
    a_max = jax.random.uniform(ks[18], (DEPTH, D_B), f32, 0.9, 0.999)
    s = a_max ** (1.0 / LRU_C)
    lam = jnp.log(s) - jnp.log1p(-s)
    w_norm_lru = 1.0 + nrm(ks[19], (DEPTH, D_B), 0.02)
    w_out = nrm(ks[20], (DEPTH, D_MIX, D_MODEL), D_MIX ** -0.5)
    w_norm_mlp = 1.0 + nrm(ks[21], (DEPTH, D_MODEL), 0.02)
    w_mlp_up = nrm(ks[22], (DEPTH, D_MODEL, D_FF), D_MODEL ** -0.5)
    w_mlp_down = nrm(ks[23], (DEPTH, D_FF, D_MODEL), D_FF ** -0.5)
    w_norm_final = 1.0 + nrm(ks[24], (D_MODEL,), 0.02)
    return {"x_prompt": x_prompt, "x_sample": x_sample,
            "state_delta": state_delta, "state_conv_delta": state_conv_delta,
            "state_lru": state_lru, "state_conv_lru": state_conv_lru,
            "w_norm_mix": w_norm_mix, "w_in": w_in, "w_conv_delta": w_conv_delta,
            "a_log": a_log, "dt_bias": dt_bias, "w_norm_delta": w_norm_delta,
            "w_conv_lru": w_conv_lru, "b_conv_lru": b_conv_lru,
            "w_gate_a": w_gate_a, "b_gate_a": b_gate_a, "w_gate_x": w_gate_x, "b_gate_x": b_gate_x,
            "lam": lam, "w_norm_lru": w_norm_lru, "w_out": w_out,
            "w_norm_mlp": w_norm_mlp, "w_mlp_up": w_mlp_up, "w_mlp_down": w_mlp_down,
            "w_norm_final": w_norm_final}


def reference(x_prompt, x_sample, state_delta, state_conv_delta, state_lru, state_conv_lru,
              w_norm_mix, w_in, w_conv_delta, a_log, dt_bias, w_norm_delta,
              w_conv_lru, b_conv_lru, w_gate_a, b_gate_a, w_gate_x, b_gate_x,
              lam, w_norm_lru, w_out, w_norm_mlp, w_mlp_up, w_mlp_down, w_norm_final):
    layers = [dict(w_norm_mix=w_norm_mix[i], w_in=w_in[i], w_conv_delta=w_conv_delta[i],
                   a_log=a_log[i], dt_bias=dt_bias[i], w_norm_delta=w_norm_delta[i],
                   w_conv_lru=w_conv_lru[i], b_conv_lru=b_conv_lru[i],
                   w_gate_a=w_gate_a[i], b_gate_a=b_gate_a[i],
                   w_gate_x=w_gate_x[i], b_gate_x=b_gate_x[i],
                   lam=lam[i], w_norm_lru=w_norm_lru[i], w_out=w_out[i],
                   w_norm_mlp=w_norm_mlp[i], w_mlp_up=w_mlp_up[i], w_mlp_down=w_mlp_down[i])
              for i in range(DEPTH)]
    bp = x_prompt.shape[0]
    dt_x = x_prompt.dtype
    z_sd = jnp.zeros((DEPTH, bp, H_A, HEAD_DIM, HEAD_DIM), jnp.float32)
    z_cd = jnp.zeros((DEPTH, bp, CONV_W - 1, 3 * D_A), dt_x)
    z_hl = jnp.zeros((DEPTH, bp, D_B), jnp.float32)
    z_cl = jnp.zeros((DEPTH, bp, CONV_W - 1, D_B), dt_x)
    y_prompt, p_delta, p_conv_delta, p_lru, p_conv_lru = _trunk(
        x_prompt, z_sd, z_cd, z_hl, z_cl, layers, w_norm_final)
    y_sample, s_delta, s_conv_delta, s_lru, s_conv_lru = _trunk(
        x_sample, state_delta, state_conv_delta, state_lru, state_conv_lru, layers, w_norm_final)
    return (y_prompt, y_sample, p_delta, p_conv_delta, p_lru, p_conv_lru,
            s_delta, s_conv_delta, s_lru, s_conv_lru)
```

```python
import functools

import jax
import jax.numpy as jnp
from jax import lax
from jax.experimental import pallas as pl
from jax.experimental.pallas import tpu as pltpu

HEAD_DIM = 128
LRU_BLOCK = 128
CONV_W = 4
LRU_C = 8.0
EPS = 1e-6
LANES = 128
SUBLANES = 8
CHUNK = 128
NILPOTENT_BLOCK = 16
VMEM_LIMIT = 56 * 1024 * 1024

F32 = jnp.float32
BF16 = jnp.bfloat16


def _cparams(sem):
    return pltpu.CompilerParams(dimension_semantics=sem, vmem_limit_bytes=VMEM_LIMIT)


def _dot(a, b):
    return jnp.dot(a, b, preferred_element_type=F32)


def _dot_nt(a, b):
    return lax.dot_general(a, b, (((1,), (1,)), ((), ())), preferred_element_type=F32)


def _split_bf16(x):
    hi = x.astype(BF16)
    lo = (x - hi.astype(F32)).astype(BF16)
    return hi, lo


def _dot3(x, y):
    xh, xl = _split_bf16(x)
    yh, yl = _split_bf16(y)
    lhs = jnp.concatenate([xh, xl, xh], axis=1)
    rhs = jnp.concatenate([yh, yh, yl], axis=0)
    return _dot(lhs, rhs)


def _sigmoid(x):
    return 1.0 / (1.0 + jnp.exp(-x))


def _softplus(x):
    return jnp.maximum(x, 0.0) + jnp.log1p(jnp.exp(-jnp.abs(x)))


def _silu(x):
    return x * _sigmoid(x)


def _gelu_tanh(x):
    c = 0.7978845608028654
    return 0.5 * x * (1.0 + jnp.tanh(c * (x + 0.044715 * (x * x * x))))


def _rms_rows(x, w):
    ms = jnp.mean(x * x, axis=-1, keepdims=True)
    return (x * lax.rsqrt(ms + EPS)) * w


def _norm_proj_kernel(x_ref, wn_ref, w_ref, wg_ref, o_ref, g_ref, xn_ref):
    @pl.when(pl.program_id(1) == 0)
    def _():
        xn_ref[...] = _rms_rows(x_ref[...], wn_ref[...]).astype(BF16)
        g_ref[...] = _dot(xn_ref[...], wg_ref[...])

    o_ref[...] = _dot(xn_ref[...], w_ref[...])


def _norm_proj(x, wn, w_main, w_gate, tm, tn):
    t, d = x.shape
    n = w_main.shape[1]
    return pl.pallas_call(
        _norm_proj_kernel,
        grid=(t // tm, n // tn),
        in_specs=[
            pl.BlockSpec((tm, d), lambda i, j: (i, 0)),
            pl.BlockSpec((1, d), lambda i, j: (0, 0)),
            pl.BlockSpec((d, tn), lambda i, j: (0, j)),
            pl.BlockSpec((d, LANES), lambda i, j: (0, 0)),
        ],
        out_specs=[
            pl.BlockSpec((tm, tn), lambda i, j: (i, j)),
            pl.BlockSpec((tm, LANES), lambda i, j: (i, 0)),
        ],
        out_shape=[jax.ShapeDtypeStruct((t, n), F32), jax.ShapeDtypeStruct((t, LANES), F32)],
        scratch_shapes=[pltpu.VMEM((tm, d), BF16)],
        compiler_params=_cparams(("arbitrary", "arbitrary")),
        name="norm_proj",
    )(x, wn, w_main, w_gate)


def _conv_block(ext_ref, x, prev_ref, w_ref, first, rows):
    lo = SUBLANES - (CONV_W - 1)

    @pl.when(first)
    def _():
        ext_ref[lo:SUBLANES, :] = prev_ref[0]

    @pl.when(jnp.logical_not(first))
    def _():
        ext_ref[lo:SUBLANES, :] = ext_ref[rows + lo:rows + SUBLANES, :]

    ext_ref[SUBLANES:SUBLANES + rows, :] = x
    y = ext_ref[lo:lo + rows, :] * w_ref[0:1, :]
    for j in range(1, CONV_W):
        y = y + ext_ref[lo + j:lo + j + rows, :] * w_ref[j:j + 1, :]
    return y


def _unit_lower_inverse(a, row, col):
    n = a.shape[0]
    eye = (row == col).astype(F32)
    shift = NILPOTENT_BLOCK.bit_length() - 1
    same = (row >> shift) == (col >> shift)
    d = jnp.where(same, a, 0.0)
    t = eye - d
    p = d
    size = 2
    while size < NILPOTENT_BLOCK:
        p = _dot3(p, p)
        t = t + _dot3(t, p)
        size *= 2
    blk = NILPOTENT_BLOCK
    while blk < n:
        shift += 1
        nxt = (row >> shift) == (col >> shift)
        off = jnp.where(jnp.logical_and(nxt, jnp.logical_not(same)), a, 0.0)
        t = t - _dot3(_dot3(t, off), t)
        same = nxt
        blk *= 2
    return t


def _delta_kernel(q_ref, k_ref, v_ref, z_ref, g_ref, pq_ref, pk_ref, pv_ref, s0_ref,
                  wq_ref, wk_ref, wv_ref, arow_ref, dtrow_ref, wnorm_ref,
                  o_ref, s_ref, eq_ref, ek_ref, ev_ref, st_ref, *, rows, nblk, valid, heads):
    h = pl.program_id(1)
    l = pl.program_id(2)
    first = l == 0

    @pl.when(first)
    def _():
        st_ref[...] = s0_ref[0, 0]

    q = _silu(_conv_block(eq_ref, q_ref[0], pq_ref, wq_ref, first, rows))
    k = _silu(_conv_block(ek_ref, k_ref[0], pk_ref, wk_ref, first, rows))
    v = _silu(_conv_block(ev_ref, v_ref[0], pv_ref, wv_ref, first, rows))
    q = q * lax.rsqrt(jnp.sum(q * q, axis=-1, keepdims=True) + EPS) * (HEAD_DIM ** -0.5)
    k = k * lax.rsqrt(jnp.sum(k * k, axis=-1, keepdims=True) + EPS)

    gates = g_ref[...]
    lane = lax.broadcasted_iota(jnp.int32, gates.shape, 1)
    beta_all = _sigmoid(gates)
    g_all = -jnp.exp(arow_ref[...]) * _softplus(gates + dtrow_ref[...])
    beta = jnp.sum(jnp.where(lane == h, beta_all, 0.0), axis=1, keepdims=True)
    g = jnp.sum(jnp.where(lane == heads + h, g_all, 0.0), axis=1, keepdims=True)
    if valid < rows * nblk:
        t_idx = l * rows + lax.broadcasted_iota(jnp.int32, (rows, 1), 0)
        live = t_idx < valid
        beta = jnp.where(live, beta, 0.0)
        g = jnp.where(live, g, 0.0)
        k = jnp.where(live, k, 0.0)

    zg = _silu(z_ref[0])
    row = lax.broadcasted_iota(jnp.int32, (CHUNK, CHUNK), 0)
    col = lax.broadcasted_iota(jnp.int32, (CHUNK, CHUNK), 1)
    causal = row >= col
    strict = row > col
    tril = causal.astype(F32)

    for c in range(rows // CHUNK):
        sl = slice(c * CHUNK, (c + 1) * CHUNK)
        qc, kc, vc = q[sl], k[sl], v[sl]
        beta_b = jnp.broadcast_to(beta[sl], (CHUNK, CHUNK))
        g_b = jnp.broadcast_to(g[sl], (CHUNK, CHUNK))
        gcum = jnp.dot(tril, g_b, preferred_element_type=F32, precision=lax.Precision.HIGHEST)
        gcum_t = gcum.T
        decay = jnp.where(causal, jnp.exp(jnp.where(causal, gcum - gcum_t, 0.0)), 0.0)
        eg = jnp.exp(gcum)
        kb = kc * beta_b
        kc16 = kc.astype(BF16)
        a_mat = jnp.where(strict, _dot_nt(kb.astype(BF16), kc16) * decay, 0.0)
        t_inv = _unit_lower_inverse(a_mat, row, col)
        rhs = jnp.concatenate([vc * beta_b, kb * eg], axis=1)
        sol = _dot3(t_inv, rhs)
        u, w = sol[:, :HEAD_DIM], sol[:, HEAD_DIM:]
        qk = _dot_nt(qc.astype(BF16), kc16) * decay
        qg = qc * eg
        g_last = gcum[CHUNK - 1:CHUNK, :]
        kd = kc * jnp.exp(g_last - gcum)

        s = st_ref[...]
        s16 = s.astype(BF16)
        v_new = u - _dot(w.astype(BF16), s16)
        v16 = v_new.astype(BF16)
        o = _dot(qg.astype(BF16), s16) + _dot(qk.astype(BF16), v16)
        st_ref[...] = s * jnp.exp(g_last) + _dot(kd.T.astype(BF16), v16)

        o = _rms_rows(o, wnorm_ref[...]) * zg[sl]
        o_ref[0, sl, :] = o.astype(o_ref.dtype)

    @pl.when(l == nblk - 1)
    def _():
        s_ref[0, 0] = st_ref[...]


def _delta_mixer(proj, gates, conv_prev, s0, w_conv, a_row, dt_row, w_norm, *, valid, rows):
    b, l, _ = proj.shape
    heads = s0.shape[1]
    nl = l // rows
    blk = lambda off: pl.BlockSpec((1, rows, HEAD_DIM), lambda bi, hi, li, off=off: (bi, li, off * heads + hi))
    prev = lambda off: pl.BlockSpec((1, CONV_W - 1, HEAD_DIM), lambda bi, hi, li, off=off: (bi, 0, off * heads + hi))
    wcv = lambda off: pl.BlockSpec((CONV_W, HEAD_DIM), lambda bi, hi, li, off=off: (0, off * heads + hi))
    row_spec = pl.BlockSpec((1, LANES), lambda bi, hi, li: (0, 0))
    state_spec = pl.BlockSpec((1, 1, HEAD_DIM, HEAD_DIM), lambda bi, hi, li: (bi, hi, 0, 0))
    kern = functools.partial(_delta_kernel, rows=rows, nblk=nl, valid=valid, heads=heads)
    return pl.pallas_call(
        kern,
        grid=(b, heads, nl),
        in_specs=[blk(0), blk(1), blk(2), blk(3),
                  pl.BlockSpec((rows, LANES), lambda bi, hi, li: (bi * nl + li, 0)),
                  prev(0), prev(1), prev(2), state_spec,
                  wcv(0), wcv(1), wcv(2), row_spec, row_spec, row_spec],
        out_specs=[pl.BlockSpec((1, rows, HEAD_DIM), lambda bi, hi, li: (bi, li, hi)), state_spec],
        out_shape=[jax.ShapeDtypeStruct((b, l, heads * HEAD_DIM), BF16),
                   jax.ShapeDtypeStruct(s0.shape, F32)],
        scratch_shapes=[pltpu.VMEM((rows + SUBLANES, HEAD_DIM), F32)] * 3 + [pltpu.VMEM((HEAD_DIM, HEAD_DIM), F32)],
        compiler_params=_cparams(("arbitrary", "arbitrary", "arbitrary")),
        name="delta_mixer",
    )(proj, proj, proj, proj, gates, conv_prev, conv_prev, conv_prev, s0,
      w_conv, w_conv, w_conv, a_row, dt_row, w_norm)


def _lru_kernel(x_ref, y_ref, prev_ref, h0_ref, wc_ref, bc_ref, wa_ref, ba_ref, wx_ref, bx_ref,
                lam_ref, wn_ref, o_ref, hl_ref, ext_ref, a_ref, b_ref, h_ref, *, rows, nblk, valid, groups):
    l = pl.program_id(1)
    first = l == 0

    @pl.when(first)
    def _():
        h_ref[...] = h0_ref[0]
        a_ref[0:SUBLANES, :] = jnp.ones((SUBLANES, a_ref.shape[1]), F32)
        b_ref[0:SUBLANES, :] = jnp.zeros((SUBLANES, b_ref.shape[1]), F32)

    xc = _conv_block(ext_ref, x_ref[0], prev_ref, wc_ref, first, rows) + bc_ref[...]
    sub = lax.broadcasted_iota(jnp.int32, (rows, LRU_BLOCK), 0) & (SUBLANES - 1)
    for gi in range(groups):
        cs = slice(gi * LRU_BLOCK, (gi + 1) * LRU_BLOCK)
        xg = xc[:, cs]
        xg16 = xg.astype(BF16)
        gate_r = _sigmoid(_dot(xg16, wa_ref[gi]) + ba_ref[:, cs])
        gate_i = _sigmoid(_dot(xg16, wx_ref[gi]) + bx_ref[:, cs])
        log_a = -LRU_C * gate_r * _softplus(-lam_ref[:, cs])
        a = jnp.exp(log_a)
        bb = jnp.sqrt(1.0 - jnp.exp(2.0 * log_a)) * gate_i * xg
        for s in (1, 2, 4):
            a_ref[SUBLANES:SUBLANES + rows, cs] = a
            b_ref[SUBLANES:SUBLANES + rows, cs] = bb
            keep = sub >= s
            a_sh = jnp.where(keep, a_ref[SUBLANES - s:SUBLANES - s + rows, cs], 1.0)
            b_sh = jnp.where(keep, b_ref[SUBLANES - s:SUBLANES - s + rows, cs], 0.0)
            bb = a * b_sh + bb
            a = a * a_sh
        a_ref[SUBLANES:SUBLANES + rows, cs] = a
        b_ref[SUBLANES:SUBLANES + rows, cs] = bb

    def step(i, h_prev):
        r0 = pl.multiple_of(SUBLANES + i * SUBLANES, SUBLANES)
        hb = a_ref[pl.ds(r0, SUBLANES), :] * h_prev + b_ref[pl.ds(r0, SUBLANES), :]
        b_ref[pl.ds(r0, SUBLANES), :] = hb
        return hb[SUBLANES - 1:SUBLANES, :]

    h_ref[...] = lax.fori_loop(0, rows // SUBLANES, step, h_ref[...])

    hs = b_ref[SUBLANES:SUBLANES + rows, :]
    last_blk, last_row = divmod(valid - 1, rows)

    @pl.when(l == last_blk)
    def _():
        hl_ref[0] = hs[last_row:last_row + 1, :]

    y = _gelu_tanh(y_ref[0]) * hs
    o_ref[0] = _rms_rows(y, wn_ref[...]).astype(o_ref.dtype)


def _lru_mixer(proj, conv_prev, h0, w_conv, b_conv, w_ga, b_ga, w_gx, b_gx, lam, w_norm,
               *, valid, rows, x_blk, y_blk):
    b, l, _ = proj.shape
    d = h0.shape[-1]
    groups = d // LRU_BLOCK
    nl = l // rows
    vec = pl.BlockSpec((1, d), lambda bi, li: (0, 0))
    gw = pl.BlockSpec((groups, LRU_BLOCK, LRU_BLOCK), lambda bi, li: (0, 0, 0))
    kern = functools.partial(_lru_kernel, rows=rows, nblk=nl, valid=valid, groups=groups)
    return pl.pallas_call(
        kern,
        grid=(b, nl),
        in_specs=[pl.BlockSpec((1, rows, d), lambda bi, li: (bi, li, x_blk)),
                  pl.BlockSpec((1, rows, d), lambda bi, li: (bi, li, y_blk)),
                  pl.BlockSpec((1, CONV_W - 1, d), lambda bi, li: (bi, 0, 0)),
                  pl.BlockSpec((1, 1, d), lambda bi, li: (bi, 0, 0)),
                  pl.BlockSpec((CONV_W, d), lambda bi, li: (0, 0)),
                  vec, gw, vec, gw, vec, vec, vec],
        out_specs=[pl.BlockSpec((1, rows, d), lambda bi, li: (bi, li, 0)),
                   pl.BlockSpec((1, 1, d), lambda bi, li: (bi, 0, 0))],
        out_shape=[jax.ShapeDtypeStruct((b, l, d), BF16), jax.ShapeDtypeStruct((b, 1, d), F32)],
        scratch_shapes=[pltpu.VMEM((rows + SUBLANES, d), F32)] * 3 + [pltpu.VMEM((1, d), F32)],
        compiler_params=_cparams(("arbitrary", "arbitrary")),
        name="lru_mixer",
    )(proj, proj, conv_prev, h0, w_conv, b_conv, w_ga, b_ga, w_gx, b_gx, lam, w_norm)


def _out_proj_kernel(x_ref, a_ref, b_ref, wa_ref, wb_ref, o_ref):
    o_ref[...] = x_ref[...] + _dot(a_ref[...], wa_ref[...]) + _dot(b_ref[...], wb_ref[...])


def _out_proj(x, mix_a, mix_b, w_out, tm, tn):
    t, d = x.shape
    da, db = mix_a.shape[1], mix_b.shape[1]
    assert da == db
    return pl.pallas_call(
        _out_proj_kernel,
        grid=(t // tm, d // tn),
        in_specs=[pl.BlockSpec((tm, tn), lambda i, j: (i, j)),
                  pl.BlockSpec((tm, da), lambda i, j: (i, 0)),
                  pl.BlockSpec((tm, db), lambda i, j: (i, 0)),
                  pl.BlockSpec((da, tn), lambda i, j: (0, j)),
                  pl.BlockSpec((db, tn), lambda i, j: (1, j))],
        out_specs=pl.BlockSpec((tm, tn), lambda i, j: (i, j)),
        out_shape=jax.ShapeDtypeStruct((t, d), F32),
        compiler_params=_cparams(("arbitrary", "arbitrary")),
        name="out_proj",
    )(x, mix_a, mix_b, w_out, w_out)


def _mlp_kernel(x_ref, wn_ref, wu_ref, wd_ref, wf_ref, o_ref, xn_ref, *, final_norm):
    j = pl.program_id(1)

    @pl.when(j == 0)
    def _():
        x = x_ref[...]
        xn_ref[...] = _rms_rows(x, wn_ref[...]).astype(BF16)
        o_ref[...] = x

    hid = jnp.maximum(_dot(xn_ref[...], wu_ref[...]), 0.0)
    o_ref[...] += _dot((hid * hid).astype(BF16), wd_ref[...])

    if final_norm:
        @pl.when(j == pl.num_programs(1) - 1)
        def _():
            o_ref[...] = _rms_rows(o_ref[...], wf_ref[...])


def _mlp(x, wn, w_up, w_down, w_final, tm, tf, final_norm):
    t, d = x.shape
    f = w_up.shape[1]
    return pl.pallas_call(
        functools.partial(_mlp_kernel, final_norm=final_norm),
        grid=(t // tm, f // tf),
        in_specs=[pl.BlockSpec((tm, d), lambda i, j: (i, 0)),
                  pl.BlockSpec((1, d), lambda i, j: (0, 0)),
                  pl.BlockSpec((d, tf), lambda i, j: (0, j)),
                  pl.BlockSpec((tf, d), lambda i, j: (j, 0)),
                  pl.BlockSpec((1, d), lambda i, j: (0, 0))],
        out_specs=pl.BlockSpec((tm, d), lambda i, j: (i, 0)),
        out_shape=jax.ShapeDtypeStruct((t, d), F32),
        scratch_shapes=[pltpu.VMEM((tm, d), BF16)],
        compiler_params=_cparams(("arbitrary", "arbitrary")),
        name="mlp",
    )(x, wn, w_up, w_down, w_final)


def _tile(n, pref):
    t = min(n, pref)
    while n % t:
        t //= 2
    return t


def _prep_layer(i, p):
    d_a = p["w_conv_delta"].shape[-1] // 3
    heads = p["a_log"].shape[-1]
    w_in = p["w_in"][i]
    gate_lo, gate_hi = 4 * d_a, 4 * d_a + 2 * heads
    w_main = jnp.concatenate([w_in[:, :gate_lo], w_in[:, gate_hi:]], axis=1).astype(BF16)
    w_gate = jnp.pad(w_in[:, gate_lo:gate_hi], ((0, 0), (0, LANES - 2 * heads))).astype(BF16)
    row = lambda v: jnp.pad(v, (heads, LANES - 2 * heads))[None, :].astype(F32)
    return dict(
        w_norm_mix=p["w_norm_mix"][i][None, :], w_main=w_main, w_gate=w_gate,
        w_conv_delta=p["w_conv_delta"][i], a_row=row(p["a_log"][i]), dt_row=row(p["dt_bias"][i]),
        w_norm_delta=p["w_norm_delta"][i][None, :],
        w_conv_lru=p["w_conv_lru"][i], b_conv_lru=p["b_conv_lru"][i][None, :],
        w_gate_a=p["w_gate_a"][i].astype(BF16), b_gate_a=p["b_gate_a"][i][None, :],
        w_gate_x=p["w_gate_x"][i].astype(BF16), b_gate_x=p["b_gate_x"][i][None, :],
        lam=p["lam"][i][None, :], w_norm_lru=p["w_norm_lru"][i][None, :],
        w_out=p["w_out"][i].astype(BF16), w_norm_mlp=p["w_norm_mlp"][i][None, :],
        w_mlp_up=p["w_mlp_up"][i].astype(BF16), w_mlp_down=p["w_mlp_down"][i].astype(BF16),
    )


def _layer(x, conv_d, s_d, conv_l, h_l, p, w_final, final_norm):
    b, l, d = x.shape
    t = b * l
    d_a = p["w_conv_delta"].shape[-1] // 3
    d_b = p["w_conv_lru"].shape[-1]
    assert d_a == d_b and d_a % HEAD_DIM == 0
    heads = d_a // HEAD_DIM
    tm = _tile(t, 512)
    xf = x.reshape(t, d)

    proj, gates = _norm_proj(xf, p["w_norm_mix"], p["w_main"], p["w_gate"], tm, _tile(p["w_main"].shape[1], 1024))
    n_main = proj.shape[1]
    proj = proj.reshape(b, l, n_main)
    conv_d_new = jnp.concatenate([conv_d, proj[:, :, :3 * d_a]], axis=1)[:, -(CONV_W - 1):]
    conv_l_new = jnp.concatenate([conv_l, proj[:, :, 4 * d_a:4 * d_a + d_b]], axis=1)[:, -(CONV_W - 1):]

    lp = -(-l // CHUNK) * CHUNK
    if lp != l:
        proj = jnp.pad(proj, ((0, 0), (0, lp - l), (0, 0)))
        gates = jnp.pad(gates.reshape(b, l, LANES), ((0, 0), (0, lp - l), (0, 0))).reshape(b * lp, LANES)

    delta_out, s_d_new = _delta_mixer(
        proj, gates, conv_d, s_d, p["w_conv_delta"], p["a_row"], p["dt_row"], p["w_norm_delta"],
        valid=l, rows=_tile(lp, 512))
    lru_out, h_new = _lru_mixer(
        proj, conv_l, h_l[:, None, :], p["w_conv_lru"], p["b_conv_lru"], p["w_gate_a"], p["b_gate_a"],
        p["w_gate_x"], p["b_gate_x"], p["lam"], p["w_norm_lru"],
        valid=l, rows=_tile(lp, 256), x_blk=4 * d_a // d_b, y_blk=4 * d_a // d_b + 1)
    if lp != l:
        delta_out, lru_out = delta_out[:, :l], lru_out[:, :l]

    x1 = _out_proj(xf, delta_out.reshape(t, d_a), lru_out.reshape(t, d_b), p["w_out"], tm, _tile(d, 1024))
    x2 = _mlp(x1, p["w_norm_mlp"], p["w_mlp_up"], p["w_mlp_down"], w_final, tm,
              _tile(p["w_mlp_up"].shape[1], 256), final_norm)
    return x2.reshape(b, l, d), s_d_new, conv_d_new, h_new[:, 0, :], conv_l_new


def _trunk(x, s_d, conv_d, h_l, conv_l, layers, w_final):
    sds, cds, hls, cls = [], [], [], []
    depth = len(layers)
    for i, p in enumerate(layers):
        x, sd, cd, hl, cl = _layer(x, conv_d[i], s_d[i], conv_l[i], h_l[i], p, w_final, i == depth - 1)
        sds.append(sd)
        cds.append(cd)
        hls.append(hl)
        cls.append(cl)
    return x, jnp.stack(sds), jnp.stack(cds), jnp.stack(hls), jnp.stack(cls)


def kernel(x_prompt, x_sample, state_delta, state_conv_delta, state_lru, state_conv_lru,
           w_norm_mix, w_in, w_conv_delta, a_log, dt_bias, w_norm_delta,
           w_conv_lru, b_conv_lru, w_gate_a, b_gate_a, w_gate_x, b_gate_x,
           lam, w_norm_lru, w_out, w_norm_mlp, w_mlp_up, w_mlp_down, w_norm_final):
    params = dict(w_norm_mix=w_norm_mix, w_in=w_in, w_conv_delta=w_conv_delta, a_log=a_log, dt_bias=dt_bias,
                  w_norm_delta=w_norm_delta, w_conv_lru=w_conv_lru, b_conv_lru=b_conv_lru,
                  w_gate_a=w_gate_a, b_gate_a=b_gate_a, w_gate_x=w_gate_x, b_gate_x=b_gate_x,
                  lam=lam, w_norm_lru=w_norm_lru, w_out=w_out, w_norm_mlp=w_norm_mlp,
                  w_mlp_up=w_mlp_up, w_mlp_down=w_mlp_down)
    depth = w_in.shape[0]
    layers = [_prep_layer(i, params) for i in range(depth)]
    w_final = w_norm_final[None, :]

    bp = x_prompt.shape[0]
    z_sd = jnp.zeros((depth, bp) + state_delta.shape[2:], F32)
    z_cd = jnp.zeros((depth, bp) + state_conv_delta.shape[2:], x_prompt.dtype)
    z_hl = jnp.zeros((depth, bp) + state_lru.shape[2:], F32)
    z_cl = jnp.zeros((depth, bp) + state_conv_lru.shape[2:], x_prompt.dtype)
    outs_p = _trunk(x_prompt, z_sd, z_cd, z_hl, z_cl, layers, w_final)
    outs_s = _trunk(x_sample, state_delta, state_conv_delta, state_lru, state_conv_lru, layers, w_final)
    return (outs_p[0], outs_s[0]) + outs_p[1:] + outs_s[1:]
```

```python
import functools

import jax
import jax.numpy as jnp
from jax import lax
from jax.experimental import pallas as pl
from jax.experimental.pallas import tpu as pltpu

HEAD_DIM = 128
LRU_BLOCK = 128
CONV_W = 4
LRU_C = 8.0
EPS = 1e-6
LANES = 128
SUBLANES = 8
CHUNK = 128
NILPOTENT_BLOCK = 16
VMEM_LIMIT = 56 * 1024 * 1024

F32 = jnp.float32
BF16 = jnp.bfloat16


def _cparams(sem):
    return pltpu.CompilerParams(dimension_semantics=sem, vmem_limit_bytes=VMEM_LIMIT)


def _dot(a, b):
    return jnp.dot(a, b, preferred_element_type=F32)


def _dot_nt(a, b):
    return lax.dot_general(a, b, (((1,), (1,)), ((), ())), preferred_element_type=F32)


def _split_bf16(x):
    hi = x.astype(BF16)
    lo = (x - hi.astype(F32)).astype(BF16)
    return hi, lo


def _dot3(xs, ys):
    (xh, xl), (yh, yl) = xs, ys
    lhs = jnp.concatenate([xh, xl, xh], axis=1)
    rhs = jnp.concatenate([yh, yh, yl], axis=0)
    return _dot(lhs, rhs)


def _cumsum_rows(tril16, x):
    hi, lo = _split_bf16(x)
    lo2 = (x - hi.astype(F32) - lo.astype(F32)).astype(BF16)
    lhs = jnp.concatenate([tril16, tril16, tril16], axis=1)
    rhs = jnp.concatenate([hi, lo, lo2], axis=0)
    return _dot(lhs, rhs)


def _sigmoid(x):
    return 1.0 / (1.0 + jnp.exp(-x))


def _softplus(x):
    return jnp.maximum(x, 0.0) + jnp.log1p(jnp.exp(-jnp.abs(x)))


def _silu(x):
    return x * _sigmoid(x)


def _gelu_tanh(x):
    c = 0.7978845608028654
    return 0.5 * x * (1.0 + jnp.tanh(c * (x + 0.044715 * (x * x * x))))


def _rms_rows(x, w):
    ms = jnp.mean(x * x, axis=-1, keepdims=True)
    return (x * lax.rsqrt(ms + EPS)) * w


def _norm_proj_kernel(x_ref, wn_ref, w_ref, wg_ref, o_ref, g_ref, xn_ref):
    @pl.when(pl.program_id(1) == 0)
    def _():
        xn_ref[...] = _rms_rows(x_ref[...], wn_ref[...]).astype(BF16)
        g_ref[...] = _dot(xn_ref[...], wg_ref[...])

    o_ref[...] = _dot(xn_ref[...], w_ref[...])


def _norm_proj(x, wn, w_main, w_gate, tm, tn):
    t, d = x.shape
    n = w_main.shape[1]
    return pl.pallas_call(
        _norm_proj_kernel,
        grid=(t // tm, n // tn),
        in_specs=[
            pl.BlockSpec((tm, d), lambda i, j: (i, 0)),
            pl.BlockSpec((1, d), lambda i, j: (0, 0)),
            pl.BlockSpec((d, tn), lambda i, j: (0, j)),
            pl.BlockSpec((d, LANES), lambda i, j: (0, 0)),
        ],
        out_specs=[
            pl.BlockSpec((tm, tn), lambda i, j: (i, j)),
            pl.BlockSpec((tm, LANES), lambda i, j: (i, 0)),
        ],
        out_shape=[jax.ShapeDtypeStruct((t, n), F32), jax.ShapeDtypeStruct((t, LANES), F32)],
        scratch_shapes=[pltpu.VMEM((tm, d), BF16)],
        compiler_params=_cparams(("arbitrary", "arbitrary")),
        name="norm_proj",
    )(x, wn, w_main, w_gate)


def _conv_block(ext_ref, x, prev_ref, w_ref, first, rows):
    lo = SUBLANES - (CONV_W - 1)

    @pl.when(first)
    def _():
        ext_ref[lo:SUBLANES, :] = prev_ref[0]

    @pl.when(jnp.logical_not(first))
    def _():
        ext_ref[lo:SUBLANES, :] = ext_ref[rows + lo:rows + SUBLANES, :]

    ext_ref[SUBLANES:SUBLANES + rows, :] = x
    y = ext_ref[lo:lo + rows, :] * w_ref[0:1, :]
    for j in range(1, CONV_W):
        y = y + ext_ref[lo + j:lo + j + rows, :] * w_ref[j:j + 1, :]
    return y


def _unit_lower_inverses(mats, row, col):
    n = mats[0].shape[0]
    eye = (row == col).astype(F32)
    shift = NILPOTENT_BLOCK.bit_length() - 1
    same = (row >> shift) == (col >> shift)
    ps = [jnp.where(same, a, 0.0) for a in mats]
    ts = [eye - p for p in ps]
    pss = [_split_bf16(p) for p in ps]
    size = 2
    while size < NILPOTENT_BLOCK:
        pss = [_split_bf16(_dot3(p, p)) for p in pss]
        ts = [t + _dot3(_split_bf16(t), p) for t, p in zip(ts, pss)]
        size *= 2
    blk = NILPOTENT_BLOCK
    tss = [_split_bf16(t) for t in ts]
    while blk < n:
        shift += 1
        nxt = (row >> shift) == (col >> shift)
        sel = jnp.logical_and(nxt, jnp.logical_not(same))
        nss = [_split_bf16(_dot3(t, _split_bf16(jnp.where(sel, a, 0.0)))) for t, a in zip(tss, mats)]
        ts = [t - _dot3(nm, t2) for t, nm, t2 in zip(ts, nss, tss)]
        tss = [_split_bf16(t) for t in ts]
        same = nxt
        blk *= 2
    return tss


def _delta_kernel(q_ref, k_ref, v_ref, z_ref, g_ref, pq_ref, pk_ref, pv_ref, s0_ref,
                  wq_ref, wk_ref, wv_ref, arow_ref, dtrow_ref, wnorm_ref,
                  o_ref, s_ref, eq_ref, ek_ref, ev_ref, st_ref, *, rows, nblk, valid, heads):
    h = pl.program_id(1)
    l = pl.program_id(2)
    first = l == 0

    @pl.when(first)
    def _():
        st_ref[...] = s0_ref[0, 0]

    q = _silu(_conv_block(eq_ref, q_ref[0], pq_ref, wq_ref, first, rows))
    k = _silu(_conv_block(ek_ref, k_ref[0], pk_ref, wk_ref, first, rows))
    v = _silu(_conv_block(ev_ref, v_ref[0], pv_ref, wv_ref, first, rows))
    q = q * lax.rsqrt(jnp.sum(q * q, axis=-1, keepdims=True) + EPS) * (HEAD_DIM ** -0.5)
    k = k * lax.rsqrt(jnp.sum(k * k, axis=-1, keepdims=True) + EPS)

    gates = g_ref[...]
    lane = lax.broadcasted_iota(jnp.int32, gates.shape, 1)
    beta_all = _sigmoid(gates)
    g_all = -jnp.exp(arow_ref[...]) * _softplus(gates + dtrow_ref[...])
    beta = jnp.sum(jnp.where(lane == h, beta_all, 0.0), axis=1, keepdims=True)
    g = jnp.sum(jnp.where(lane == heads + h, g_all, 0.0), axis=1, keepdims=True)
    if valid < rows * nblk:
        t_idx = l * rows + lax.broadcasted_iota(jnp.int32, (rows, 1), 0)
        live = t_idx < valid
        beta = jnp.where(live, beta, 0.0)
        g = jnp.where(live, g, 0.0)
        k = jnp.where(live, k, 0.0)

    zg = _silu(z_ref[0])
    row = lax.broadcasted_iota(jnp.int32, (CHUNK, CHUNK), 0)
    col = lax.broadcasted_iota(jnp.int32, (CHUNK, CHUNK), 1)
    causal = row >= col
    strict = row > col
    tril16 = causal.astype(BF16)

    pre = []
    for c in range(rows // CHUNK):
        sl = slice(c * CHUNK, (c + 1) * CHUNK)
        qc, kc, vc = q[sl], k[sl], v[sl]
        beta_b = jnp.broadcast_to(beta[sl], (CHUNK, CHUNK))
        g_b = jnp.broadcast_to(g[sl], (CHUNK, CHUNK))
        gcum = _cumsum_rows(tril16, g_b)
        decay = jnp.where(causal, jnp.exp(jnp.where(causal, gcum - gcum.T, 0.0)), 0.0)
        eg = jnp.exp(gcum)
        kb = kc * beta_b
        kc16 = kc.astype(BF16)
        a_mat = jnp.where(strict, _dot_nt(kb.astype(BF16), kc16) * decay, 0.0)
        g_last = gcum[CHUNK - 1:CHUNK, :]
        pre.append(dict(
            a=a_mat, rhs=jnp.concatenate([vc * beta_b, kb * eg], axis=1),
            qk=(_dot_nt(qc.astype(BF16), kc16) * decay).astype(BF16), qg=(qc * eg).astype(BF16),
            kdt=(kc * jnp.exp(g_last - gcum)).T.astype(BF16), gl=jnp.exp(g_last)))
    t_invs = _unit_lower_inverses([p["a"] for p in pre], row, col)
    sols = [_dot3(t, _split_bf16(p["rhs"])) for t, p in zip(t_invs, pre)]

    s = st_ref[...]
    for c, (p, sol) in enumerate(zip(pre, sols)):
        sl = slice(c * CHUNK, (c + 1) * CHUNK)
        s16 = s.astype(BF16)
        v_new = sol[:, :HEAD_DIM] - _dot(sol[:, HEAD_DIM:].astype(BF16), s16)
        v16 = v_new.astype(BF16)
        o = _dot(p["qg"], s16) + _dot(p["qk"], v16)
        s = s * p["gl"] + _dot(p["kdt"], v16)
        o_ref[0, sl, :] = (_rms_rows(o, wnorm_ref[...]) * zg[sl]).astype(o_ref.dtype)
    st_ref[...] = s

    @pl.when(l == nblk - 1)
    def _():
        s_ref[0, 0] = st_ref[...]


def _delta_mixer(proj, gates, conv_prev, s0, w_conv, a_row, dt_row, w_norm, *, valid, rows):
    b, l, _ = proj.shape
    heads = s0.shape[1]
    nl = l // rows
    blk = lambda off: pl.BlockSpec((1, rows, HEAD_DIM), lambda bi, hi, li, off=off: (bi, li, off * heads + hi))
    prev = lambda off: pl.BlockSpec((1, CONV_W - 1, HEAD_DIM), lambda bi, hi, li, off=off: (bi, 0, off * heads + hi))
    wcv = lambda off: pl.BlockSpec((CONV_W, HEAD_DIM), lambda bi, hi, li, off=off: (0, off * heads + hi))
    row_spec = pl.BlockSpec((1, LANES), lambda bi, hi, li: (0, 0))
    state_spec = pl.BlockSpec((1, 1, HEAD_DIM, HEAD_DIM), lambda bi, hi, li: (bi, hi, 0, 0))
    kern = functools.partial(_delta_kernel, rows=rows, nblk=nl, valid=valid, heads=heads)
    return pl.pallas_call(
        kern,
        grid=(b, heads, nl),
        in_specs=[blk(0), blk(1), blk(2), blk(3),
                  pl.BlockSpec((rows, LANES), lambda bi, hi, li: (bi * nl + li, 0)),
                  prev(0), prev(1), prev(2), state_spec,
                  wcv(0), wcv(1), wcv(2), row_spec, row_spec, row_spec],
        out_specs=[pl.BlockSpec((1, rows, HEAD_DIM), lambda bi, hi, li: (bi, li, hi)), state_spec],
        out_shape=[jax.ShapeDtypeStruct((b, l, heads * HEAD_DIM), BF16),
                   jax.ShapeDtypeStruct(s0.shape, F32)],
        scratch_shapes=[pltpu.VMEM((rows + SUBLANES, HEAD_DIM), F32)] * 3 + [pltpu.VMEM((HEAD_DIM, HEAD_DIM), F32)],
        compiler_params=_cparams(("arbitrary", "arbitrary", "arbitrary")),
        name="delta_mixer",
    )(proj, proj, proj, proj, gates, conv_prev, conv_prev, conv_prev, s0,
      w_conv, w_conv, w_conv, a_row, dt_row, w_norm)


def _lru_kernel(x_ref, y_ref, prev_ref, h0_ref, wc_ref, bc_ref, wa_ref, ba_ref, wx_ref, bx_ref,
                lam_ref, wn_ref, o_ref, hl_ref, ext_ref, a_ref, b_ref, h_ref, *, rows, valid, groups):
    l = pl.program_id(1)
    first = l == 0

    @pl.when(first)
    def _():
        h_ref[...] = h0_ref[0]
        a_ref[0:SUBLANES, :] = jnp.ones((SUBLANES, a_ref.shape[1]), F32)
        b_ref[0:SUBLANES, :] = jnp.zeros((SUBLANES, b_ref.shape[1]), F32)

    xc = _conv_block(ext_ref, x_ref[0], prev_ref, wc_ref, first, rows) + bc_ref[...]
    sub = lax.broadcasted_iota(jnp.int32, (rows, LRU_BLOCK), 0) & (SUBLANES - 1)
    for gi in range(groups):
        cs = slice(gi * LRU_BLOCK, (gi + 1) * LRU_BLOCK)
        xg = xc[:, cs]
        xg16 = xg.astype(BF16)
        gate_r = _sigmoid(_dot(xg16, wa_ref[gi]) + ba_ref[:, cs])
        gate_i = _sigmoid(_dot(xg16, wx_ref[gi]) + bx_ref[:, cs])
        log_a = -LRU_C * gate_r * _softplus(-lam_ref[:, cs])
        a = jnp.exp(log_a)
        bb = jnp.sqrt(1.0 - jnp.exp(2.0 * log_a)) * gate_i * xg
        for s in (1, 2, 4):
            a_ref[SUBLANES:SUBLANES + rows, cs] = a
            b_ref[SUBLANES:SUBLANES + rows, cs] = bb
            keep = sub >= s
            a_sh = jnp.where(keep, a_ref[SUBLANES - s:SUBLANES - s + rows, cs], 1.0)
            b_sh = jnp.where(keep, b_ref[SUBLANES - s:SUBLANES - s + rows, cs], 0.0)
            bb = a * b_sh + bb
            a = a * a_sh
        a_ref[SUBLANES:SUBLANES + rows, cs] = a
        b_ref[SUBLANES:SUBLANES + rows, cs] = bb

    def step(i, h_prev):
        r0 = pl.multiple_of(SUBLANES + i * SUBLANES, SUBLANES)
        hb = a_ref[pl.ds(r0, SUBLANES), :] * h_prev + b_ref[pl.ds(r0, SUBLANES), :]
        b_ref[pl.ds(r0, SUBLANES), :] = hb
        return hb[SUBLANES - 1:SUBLANES, :]

    h_ref[...] = lax.fori_loop(0, rows // SUBLANES, step, h_ref[...])

    hs = b_ref[SUBLANES:SUBLANES + rows, :]
    last_blk, last_row = divmod(valid - 1, rows)

    @pl.when(l == last_blk)
    def _():
        hl_ref[0] = hs[last_row:last_row + 1, :]

    y = _gelu_tanh(y_ref[0]) * hs
    o_ref[0] = _rms_rows(y, wn_ref[...]).astype(o_ref.dtype)


def _lru_mixer(proj, conv_prev, h0, w_conv, b_conv, w_ga, b_ga, w_gx, b_gx, lam, w_norm,
               *, valid, rows, x_blk, y_blk):
    b, l, _ = proj.shape
    d = h0.shape[-1]
    groups = d // LRU_BLOCK
    nl = l // rows
    vec = pl.BlockSpec((1, d), lambda bi, li: (0, 0))
    gw = pl.BlockSpec((groups, LRU_BLOCK, LRU_BLOCK), lambda bi, li: (0, 0, 0))
    kern = functools.partial(_lru_kernel, rows=rows, valid=valid, groups=groups)
    return pl.pallas_call(
        kern,
        grid=(b, nl),
        in_specs=[pl.BlockSpec((1, rows, d), lambda bi, li: (bi, li, x_blk)),
                  pl.BlockSpec((1, rows, d), lambda bi, li: (bi, li, y_blk)),
                  pl.BlockSpec((1, CONV_W - 1, d), lambda bi, li: (bi, 0, 0)),
                  pl.BlockSpec((1, 1, d), lambda bi, li: (bi, 0, 0)),
                  pl.BlockSpec((CONV_W, d), lambda bi, li: (0, 0)),
                  vec, gw, vec, gw, vec, vec, vec],
        out_specs=[pl.BlockSpec((1, rows, d), lambda bi, li: (bi, li, 0)),
                   pl.BlockSpec((1, 1, d), lambda bi, li: (bi, 0, 0))],
        out_shape=[jax.ShapeDtypeStruct((b, l, d), BF16), jax.ShapeDtypeStruct((b, 1, d), F32)],
        scratch_shapes=[pltpu.VMEM((rows + SUBLANES, d), F32)] * 3 + [pltpu.VMEM((1, d), F32)],
        compiler_params=_cparams(("arbitrary", "arbitrary")),
        name="lru_mixer",
    )(proj, proj, conv_prev, h0, w_conv, b_conv, w_ga, b_ga, w_gx, b_gx, lam, w_norm)


def _out_proj_kernel(x_ref, a_ref, b_ref, wa_ref, wb_ref, o_ref):
    o_ref[...] = x_ref[...] + _dot(a_ref[...], wa_ref[...]) + _dot(b_ref[...], wb_ref[...])


def _out_proj(x, mix_a, mix_b, w_out, tm, tn):
    t, d = x.shape
    da, db = mix_a.shape[1], mix_b.shape[1]
    assert da == db
    return pl.pallas_call(
        _out_proj_kernel,
        grid=(t // tm, d // tn),
        in_specs=[pl.BlockSpec((tm, tn), lambda i, j: (i, j)),
                  pl.BlockSpec((tm, da), lambda i, j: (i, 0)),
                  pl.BlockSpec((tm, db), lambda i, j: (i, 0)),
                  pl.BlockSpec((da, tn), lambda i, j: (0, j)),
                  pl.BlockSpec((db, tn), lambda i, j: (1, j))],
        out_specs=pl.BlockSpec((tm, tn), lambda i, j: (i, j)),
        out_shape=jax.ShapeDtypeStruct((t, d), F32),
        compiler_params=_cparams(("arbitrary", "arbitrary")),
        name="out_proj",
    )(x, mix_a, mix_b, w_out, w_out)


def _mlp_up_kernel(x_ref, wn_ref, wu_ref, h_ref, xn_ref):
    @pl.when(pl.program_id(1) == 0)
    def _():
        xn_ref[...] = _rms_rows(x_ref[...], wn_ref[...]).astype(BF16)

    hid = jnp.maximum(_dot(xn_ref[...], wu_ref[...]), 0.0)
    h_ref[...] = (hid * hid).astype(BF16)


def _mlp_up(x, wn, w_up, tm, tn):
    t, d = x.shape
    f = w_up.shape[1]
    return pl.pallas_call(
        _mlp_up_kernel,
        grid=(t // tm, f // tn),
        in_specs=[pl.BlockSpec((tm, d), lambda i, j: (i, 0)),
                  pl.BlockSpec((1, d), lambda i, j: (0, 0)),
                  pl.BlockSpec((d, tn), lambda i, j: (0, j))],
        out_specs=pl.BlockSpec((tm, tn), lambda i, j: (i, j)),
        out_shape=jax.ShapeDtypeStruct((t, f), BF16),
        scratch_shapes=[pltpu.VMEM((tm, d), BF16)],
        compiler_params=_cparams(("arbitrary", "arbitrary")),
        name="mlp_up",
    )(x, wn, w_up)


def _mlp_down_kernel(x_ref, h_ref, wd_ref, o_ref):
    @pl.when(pl.program_id(2) == 0)
    def _():
        o_ref[...] = x_ref[...]

    o_ref[...] += _dot(h_ref[...], wd_ref[...])


def _mlp_down(x, hid, w_down, tm, tn, tk):
    t, d = x.shape
    f = hid.shape[1]
    return pl.pallas_call(
        _mlp_down_kernel,
        grid=(t // tm, d // tn, f // tk),
        in_specs=[pl.BlockSpec((tm, tn), lambda i, j, k: (i, j)),
                  pl.BlockSpec((tm, tk), lambda i, j, k: (i, k)),
                  pl.BlockSpec((tk, tn), lambda i, j, k: (k, j))],
        out_specs=pl.BlockSpec((tm, tn), lambda i, j, k: (i, j)),
        out_shape=jax.ShapeDtypeStruct((t, d), F32),
        compiler_params=_cparams(("arbitrary", "arbitrary", "arbitrary")),
        name="mlp_down",
    )(x, hid, w_down)


def _final_norm_kernel(x_ref, w_ref, o_ref):
    o_ref[...] = _rms_rows(x_ref[...], w_ref[...])


def _final_norm(x, w, tm):
    t, d = x.shape
    return pl.pallas_call(
        _final_norm_kernel,
        grid=(t // tm,),
        in_specs=[pl.BlockSpec((tm, d), lambda i: (i, 0)), pl.BlockSpec((1, d), lambda i: (0, 0))],
        out_specs=pl.BlockSpec((tm, d), lambda i: (i, 0)),
        out_shape=jax.ShapeDtypeStruct((t, d), F32),
        compiler_params=_cparams(("arbitrary",)),
        name="final_norm",
    )(x, w)


def _tile(n, pref):
    t = min(n, pref)
    while n % t:
        t //= 2
    return t


def _prep_layer(i, p):
    d_a = p["w_conv_delta"].shape[-1] // 3
    heads = p["a_log"].shape[-1]
    w_in = p["w_in"][i]
    gate_lo, gate_hi = 4 * d_a, 4 * d_a + 2 * heads
    w_main = jnp.concatenate([w_in[:, :gate_lo], w_in[:, gate_hi:]], axis=1).astype(BF16)
    w_gate = jnp.pad(w_in[:, gate_lo:gate_hi], ((0, 0), (0, LANES - 2 * heads))).astype(BF16)
    row = lambda v: jnp.pad(v, (heads, LANES - 2 * heads))[None, :].astype(F32)
    return dict(
        w_norm_mix=p["w_norm_mix"][i][None, :], w_main=w_main, w_gate=w_gate,
        w_conv_delta=p["w_conv_delta"][i], a_row=row(p["a_log"][i]), dt_row=row(p["dt_bias"][i]),
        w_norm_delta=p["w_norm_delta"][i][None, :],
        w_conv_lru=p["w_conv_lru"][i], b_conv_lru=p["b_conv_lru"][i][None, :],
        w_gate_a=p["w_gate_a"][i].astype(BF16), b_gate_a=p["b_gate_a"][i][None, :],
        w_gate_x=p["w_gate_x"][i].astype(BF16), b_gate_x=p["b_gate_x"][i][None, :],
        lam=p["lam"][i][None, :], w_norm_lru=p["w_norm_lru"][i][None, :],
        w_out=p["w_out"][i].astype(BF16), w_norm_mlp=p["w_norm_mlp"][i][None, :],
        w_mlp_up=p["w_mlp_up"][i].astype(BF16), w_mlp_down=p["w_mlp_down"][i].astype(BF16),
    )


def _layer(x, conv_d, s_d, conv_l, h_l, p, w_final, final_norm):
    b, l, d = x.shape
    t = b * l
    d_a = p["w_conv_delta"].shape[-1] // 3
    d_b = p["w_conv_lru"].shape[-1]
    assert d_a == d_b and d_a % HEAD_DIM == 0
    heads = d_a // HEAD_DIM
    tm = _tile(t, 512)
    xf = x.reshape(t, d)

    proj, gates = _norm_proj(xf, p["w_norm_mix"], p["w_main"], p["w_gate"], tm, _tile(p["w_main"].shape[1], 1024))
    n_main = proj.shape[1]
    proj = proj.reshape(b, l, n_main)
    conv_d_new = jnp.concatenate([conv_d, proj[:, :, :3 * d_a]], axis=1)[:, -(CONV_W - 1):]
    conv_l_new = jnp.concatenate([conv_l, proj[:, :, 4 * d_a:4 * d_a + d_b]], axis=1)[:, -(CONV_W - 1):]

    lp = -(-l // CHUNK) * CHUNK
    if lp != l:
        proj = jnp.pad(proj, ((0, 0), (0, lp - l), (0, 0)))
        gates = jnp.pad(gates.reshape(b, l, LANES), ((0, 0), (0, lp - l), (0, 0))).reshape(b * lp, LANES)

    delta_out, s_d_new = _delta_mixer(
        proj, gates, conv_d, s_d, p["w_conv_delta"], p["a_row"], p["dt_row"], p["w_norm_delta"],
        valid=l, rows=_tile(lp, 1024))
    lru_out, h_new = _lru_mixer(
        proj, conv_l, h_l[:, None, :], p["w_conv_lru"], p["b_conv_lru"], p["w_gate_a"], p["b_gate_a"],
        p["w_gate_x"], p["b_gate_x"], p["lam"], p["w_norm_lru"],
        valid=l, rows=_tile(lp, 256), x_blk=4 * d_a // d_b, y_blk=4 * d_a // d_b + 1)
    if lp != l:
        delta_out, lru_out = delta_out[:, :l], lru_out[:, :l]

    x1 = _out_proj(xf, delta_out.reshape(t, d_a), lru_out.reshape(t, d_b), p["w_out"], tm, _tile(d, 1024))
    f = p["w_mlp_up"].shape[1]
    hid = _mlp_up(x1, p["w_norm_mlp"], p["w_mlp_up"], tm, _tile(f, 1024))
    x2 = _mlp_down(x1, hid, p["w_mlp_down"], _tile(t, 1024), _tile(d, 1024), _tile(f, 2048))
    if final_norm:
        x2 = _final_norm(x2, w_final, tm)
    return x2.reshape(b, l, d), s_d_new, conv_d_new, h_new[:, 0, :], conv_l_new


def _trunk(x, s_d, conv_d, h_l, conv_l, layers, w_final):
    sds, cds, hls, cls = [], [], [], []
    depth = len(layers)
    for i, p in enumerate(layers):
        x, sd, cd, hl, cl = _layer(x, conv_d[i], s_d[i], conv_l[i], h_l[i], p, w_final, i == depth - 1)
        sds.append(sd)
        cds.append(cd)
        hls.append(hl)
        cls.append(cl)
    return x, jnp.stack(sds), jnp.stack(cds), jnp.stack(hls), jnp.stack(cls)


def kernel(x_prompt, x_sample, state_delta, state_conv_delta, state_lru, state_conv_lru,
           w_norm_mix, w_in, w_conv_delta, a_log, dt_bias, w_norm_delta,
           w_conv_lru, b_conv_lru, w_gate_a, b_gate_a, w_gate_x, b_gate_x,
           lam, w_norm_lru, w_out, w_norm_mlp, w_mlp_up, w_mlp_down, w_norm_final):
    params = dict(w_norm_mix=w_norm_mix, w_in=w_in, w_conv_delta=w_conv_delta, a_log=a_log, dt_bias=dt_bias,
                  w_norm_delta=w_norm_delta, w_conv_lru=w_conv_lru, b_conv_lru=b_conv_lru,
                  w_gate_a=w_gate_a, b_gate_a=b_gate_a, w_gate_x=w_gate_x, b_gate_x=b_gate_x,
                  lam=lam, w_norm_lru=w_norm_lru, w_out=w_out, w_norm_mlp=w_norm_mlp,
                  w_mlp_up=w_mlp_up, w_mlp_down=w_mlp_down)
    depth = w_in.shape[0]
    layers = [_prep_layer(i, params) for i in range(depth)]
    w_final = w_norm_final[None, :]

    bp = x_prompt.shape[0]
    z_sd = jnp.zeros((depth, bp) + state_delta.shape[2:], F32)
    z_cd = jnp.zeros((depth, bp) + state_conv_delta.shape[2:], x_prompt.dtype)
    z_hl = jnp.zeros((depth, bp) + state_lru.shape[2:], F32)
    z_cl = jnp.zeros((depth, bp) + state_conv_lru.shape[2:], x_prompt.dtype)
    outs_p = _trunk(x_prompt, z_sd, z_cd, z_hl, z_cl, layers, w_final)
    outs_s = _trunk(x_sample, state_delta, state_conv_delta, state_lru, state_conv_lru, layers, w_final)
    return (outs_p[0], outs_s[0]) + outs_p[1:] + outs_s[1:]
```

```python
import functools

import jax
import jax.numpy as jnp
from jax import lax
from jax.experimental import pallas as pl
from jax.experimental.pallas import tpu as pltpu

HEAD_DIM = 128
LRU_BLOCK = 128
CONV_W = 4
LRU_C = 8.0
EPS = 1e-6
LANES = 128
SUBLANES = 8
CHUNK = 128
NILPOTENT_BLOCK = 16
GROUP_CHUNKS = 4
SQRT_FLOOR = 1e-30
VMEM_LIMIT = 56 * 1024 * 1024

F32 = jnp.float32
BF16 = jnp.bfloat16


def _cparams(sem):
    return pltpu.CompilerParams(dimension_semantics=sem, vmem_limit_bytes=VMEM_LIMIT)


def _dot(a, b):
    return jnp.dot(a, b, preferred_element_type=F32)


def _dot_nt(a, b):
    return lax.dot_general(a, b, (((1,), (1,)), ((), ())), preferred_element_type=F32)


def _split_bf16(x):
    hi = x.astype(BF16)
    lo = (x - hi.astype(F32)).astype(BF16)
    return hi, lo


def _dot3(xs, ys):
    (xh, xl), (yh, yl) = xs, ys
    lhs = jnp.concatenate([xh, xl, xh], axis=1)
    rhs = jnp.concatenate([yh, yh, yl], axis=0)
    return _dot(lhs, rhs)


def _cumsum_rows(tril16, x):
    hi, lo = _split_bf16(x)
    lo2 = (x - hi.astype(F32) - lo.astype(F32)).astype(BF16)
    lhs = jnp.concatenate([tril16, tril16, tril16], axis=1)
    rhs = jnp.concatenate([hi, lo, lo2], axis=0)
    return _dot(lhs, rhs)


def _sigmoid(x):
    return 0.5 * jnp.tanh(0.5 * x) + 0.5


def _softplus(x):
    return jnp.maximum(x, 0.0) + jnp.log1p(jnp.exp(-jnp.abs(x)))


def _silu(x):
    return x * _sigmoid(x)


def _gelu_tanh(x):
    c = 0.7978845608028654
    return 0.5 * x * (1.0 + jnp.tanh(c * (x + 0.044715 * (x * x * x))))


def _rms_rows(x, w):
    ms = jnp.mean(x * x, axis=-1, keepdims=True)
    return (x * lax.rsqrt(ms + EPS)) * w


def _norm_proj_kernel(x_ref, wn_ref, w_ref, wg_ref, o_ref, g_ref, xn_ref):
    @pl.when(pl.program_id(1) == 0)
    def _():
        xn_ref[...] = _rms_rows(x_ref[...], wn_ref[...]).astype(BF16)
        g_ref[...] = _dot(xn_ref[...], wg_ref[...])

    o_ref[...] = _dot(xn_ref[...], w_ref[...])


def _norm_proj(x, wn, w_main, w_gate, layer, tm, tn):
    t, d = x.shape
    n = w_main.shape[2]
    return pl.pallas_call(
        _norm_proj_kernel,
        grid=(t // tm, n // tn),
        in_specs=[
            pl.BlockSpec((tm, d), lambda i, j: (i, 0)),
            pl.BlockSpec((1, d), lambda i, j: (0, 0)),
            pl.BlockSpec((None, d, tn), lambda i, j: (layer, 0, j)),
            pl.BlockSpec((None, d, LANES), lambda i, j: (layer, 0, 0)),
        ],
        out_specs=[
            pl.BlockSpec((tm, tn), lambda i, j: (i, j)),
            pl.BlockSpec((tm, LANES), lambda i, j: (i, 0)),
        ],
        out_shape=[jax.ShapeDtypeStruct((t, n), F32), jax.ShapeDtypeStruct((t, LANES), F32)],
        scratch_shapes=[pltpu.VMEM((tm, d), BF16)],
        compiler_params=_cparams(("arbitrary", "arbitrary")),
        name="norm_proj",
    )(x, wn, w_main, w_gate)


def _conv_block(ext_ref, x, prev_ref, w_ref, first, rows):
    lo = SUBLANES - (CONV_W - 1)

    @pl.when(first)
    def _():
        ext_ref[lo:SUBLANES, :] = prev_ref[0]

    @pl.when(jnp.logical_not(first))
    def _():
        ext_ref[lo:SUBLANES, :] = ext_ref[rows + lo:rows + SUBLANES, :]

    ext_ref[SUBLANES:SUBLANES + rows, :] = x
    y = ext_ref[lo:lo + rows, :] * w_ref[0:1, :]
    for j in range(1, CONV_W):
        y = y + ext_ref[lo + j:lo + j + rows, :] * w_ref[j:j + 1, :]
    return y


def _unit_lower_inverse_stages(mats, row, col, live, out):
    n = min(mats[0].shape[0], live)
    eye = (row == col).astype(F32)
    shift = NILPOTENT_BLOCK.bit_length() - 1
    same = (row >> shift) == (col >> shift)
    ps = [jnp.where(same, a, 0.0) for a in mats]
    ts = [eye - p for p in ps]
    pss = [_split_bf16(p) for p in ps]
    size = 2
    while size < NILPOTENT_BLOCK:
        pss = [_split_bf16(_dot3(p, p)) for p in pss]
        yield
        ts = [t + _dot3(_split_bf16(t), p) for t, p in zip(ts, pss)]
        yield
        size *= 2
    blk = NILPOTENT_BLOCK
    tss = [_split_bf16(t) for t in ts]
    while blk < n:
        shift += 1
        nxt = (row >> shift) == (col >> shift)
        sel = jnp.logical_and(nxt, jnp.logical_not(same))
        nss = [_split_bf16(_dot3(t, _split_bf16(jnp.where(sel, a, 0.0)))) for t, a in zip(tss, mats)]
        yield
        ts = [t - _dot3(nm, t2) for t, nm, t2 in zip(ts, nss, tss)]
        tss = [_split_bf16(t) for t in ts]
        yield
        same = nxt
        blk *= 2
    out.extend(tss)


def _interleave(main, side, every):
    for count, _ in enumerate(main, 1):
        if count % every == 0:
            next(side, None)
    for _ in side:
        pass


def _delta_kernel(q_ref, k_ref, v_ref, z_ref, g_ref, pq_ref, pk_ref, pv_ref, s0_ref,
                  wq_ref, wk_ref, wv_ref, arow_ref, dtrow_ref, wnorm_ref,
                  o_ref, s_ref, eq_ref, ek_ref, ev_ref, st_ref, *, rows, nblk, valid, heads):
    h = pl.program_id(1)
    l = pl.program_id(2)
    first = l == 0

    @pl.when(first)
    def _():
        st_ref[...] = s0_ref[0, 0]

    q = _silu(_conv_block(eq_ref, q_ref[0], pq_ref, wq_ref, first, rows))
    k = _silu(_conv_block(ek_ref, k_ref[0], pk_ref, wk_ref, first, rows))
    v = _silu(_conv_block(ev_ref, v_ref[0], pv_ref, wv_ref, first, rows))
    q = q * lax.rsqrt(jnp.sum(q * q, axis=-1, keepdims=True) + EPS) * (HEAD_DIM ** -0.5)
    k = k * lax.rsqrt(jnp.sum(k * k, axis=-1, keepdims=True) + EPS)

    gates = g_ref[...]
    lane = lax.broadcasted_iota(jnp.int32, gates.shape, 1)
    beta_all = _sigmoid(gates)
    g_all = -jnp.exp(arow_ref[...]) * _softplus(gates + dtrow_ref[...])
    beta = jnp.sum(jnp.where(lane == h, beta_all, 0.0), axis=1, keepdims=True)
    g = jnp.sum(jnp.where(lane == heads + h, g_all, 0.0), axis=1, keepdims=True)
    if valid < rows * nblk:
        t_idx = l * rows + lax.broadcasted_iota(jnp.int32, (rows, 1), 0)
        live = t_idx < valid
        beta = jnp.where(live, beta, 0.0)
        g = jnp.where(live, g, 0.0)
        k = jnp.where(live, k, 0.0)

    zg = _silu(z_ref[0])
    row = lax.broadcasted_iota(jnp.int32, (CHUNK, CHUNK), 0)
    col = lax.broadcasted_iota(jnp.int32, (CHUNK, CHUNK), 1)
    causal = row >= col
    strict = row > col
    tril16 = causal.astype(BF16)

    live = min(CHUNK, valid) if nblk * rows == CHUNK else CHUNK

    def prepare(chunks, out):
        pre = []
        for c in chunks:
            sl = slice(c * CHUNK, (c + 1) * CHUNK)
            qc, kc, vc = q[sl], k[sl], v[sl]
            beta_b = jnp.broadcast_to(beta[sl], (CHUNK, CHUNK))
            g_b = jnp.broadcast_to(g[sl], (CHUNK, CHUNK))
            gcum = _cumsum_rows(tril16, g_b)
            decay = jnp.where(causal, jnp.exp(jnp.where(causal, gcum - gcum.T, 0.0)), 0.0)
            eg = jnp.exp(gcum)
            kb = kc * beta_b
            kc16 = kc.astype(BF16)
            a_mat = jnp.where(strict, _dot_nt(kb.astype(BF16), kc16) * decay, 0.0)
            g_last = gcum[CHUNK - 1:CHUNK, :]
            pre.append(dict(
                a=a_mat, rhs=jnp.concatenate([vc * beta_b, kb * eg], axis=1),
                qk=(_dot_nt(qc.astype(BF16), kc16) * decay).astype(BF16), qg=qc * eg,
                kdt=(kc * jnp.exp(g_last - gcum)).T.astype(BF16), gl=jnp.exp(g_last)))
            yield
        t_invs = []
        yield from _unit_lower_inverse_stages([p["a"] for p in pre], row, col, live, t_invs)
        sols = [_dot3(t, _split_bf16(p["rhs"])).astype(BF16) for t, p in zip(t_invs, pre)]
        yield
        for c, p, sol in zip(chunks, pre, sols):
            ks = _dot(p["kdt"], sol)
            qs = _dot(p["qk"], sol)
            lhs = jnp.concatenate([p["qg"] - qs[:, HEAD_DIM:], ks[:, HEAD_DIM:]], axis=0).astype(BF16)
            out[c] = (lhs, qs[:, :HEAD_DIM], ks[:, :HEAD_DIM], p["gl"])
        yield

    state = [st_ref[...]]

    def recur(chunks, ops):
        for c in chunks:
            sl = slice(c * CHUNK, (c + 1) * CHUNK)
            lhs, o0, s_add, gl = ops[c]
            s = state[0]
            both = _dot(lhs, s.astype(BF16))
            o = both[:CHUNK] + o0
            state[0] = (s * gl - both[CHUNK:]) + s_add
            o_ref[0, sl, :] = (_rms_rows(o, wnorm_ref[...]) * zg[sl]).astype(o_ref.dtype)
            yield

    nchunk = rows // CHUNK
    sizes = [nchunk] if nchunk <= GROUP_CHUNKS else [nchunk // 2] + [GROUP_CHUNKS] * (nchunk // 2 // GROUP_CHUNKS)
    n_products = 2 * (NILPOTENT_BLOCK.bit_length() - 2) + 2 * ((CHUNK // NILPOTENT_BLOCK).bit_length() - 1)
    ops = {}
    pending, n_pending, start = iter(()), 1, 0
    for size in sizes:
        chunks = range(start, start + size)
        start += size
        every = max(1, (size + n_products + 2) // n_pending)
        _interleave(prepare(chunks, ops), pending, every)
        pending, n_pending = recur(chunks, ops), size
    for _ in pending:
        pass
    st_ref[...] = state[0]

    @pl.when(l == nblk - 1)
    def _():
        s_ref[0, 0] = st_ref[...]


def _delta_mixer(proj, gates, conv_prev, s0, w_conv, a_row, dt_row, w_norm, *, valid, rows):
    b, l, _ = proj.shape
    heads = s0.shape[1]
    nl = l // rows
    blk = lambda off: pl.BlockSpec((1, rows, HEAD_DIM), lambda bi, hi, li, off=off: (bi, li, off * heads + hi))
    prev = lambda off: pl.BlockSpec((1, CONV_W - 1, HEAD_DIM), lambda bi, hi, li, off=off: (bi, 0, off * heads + hi))
    wcv = lambda off: pl.BlockSpec((CONV_W, HEAD_DIM), lambda bi, hi, li, off=off: (0, off * heads + hi))
    row_spec = pl.BlockSpec((1, LANES), lambda bi, hi, li: (0, 0))
    state_spec = pl.BlockSpec((1, 1, HEAD_DIM, HEAD_DIM), lambda bi, hi, li: (bi, hi, 0, 0))
    kern = functools.partial(_delta_kernel, rows=rows, nblk=nl, valid=valid, heads=heads)
    return pl.pallas_call(
        kern,
        grid=(b, heads, nl),
        in_specs=[blk(0), blk(1), blk(2), blk(3),
                  pl.BlockSpec((rows, LANES), lambda bi, hi, li: (bi * nl + li, 0)),
                  prev(0), prev(1), prev(2), state_spec,
                  wcv(0), wcv(1), wcv(2), row_spec, row_spec, row_spec],
        out_specs=[pl.BlockSpec((1, rows, HEAD_DIM), lambda bi, hi, li: (bi, li, hi)), state_spec],
        out_shape=[jax.ShapeDtypeStruct((b, l, heads * HEAD_DIM), BF16),
                   jax.ShapeDtypeStruct(s0.shape, F32)],
        scratch_shapes=[pltpu.VMEM((rows + SUBLANES, HEAD_DIM), F32)] * 3 + [pltpu.VMEM((HEAD_DIM, HEAD_DIM), F32)],
        compiler_params=_cparams(("arbitrary", "arbitrary", "arbitrary")),
        name="delta_mixer",
    )(proj, proj, proj, proj, gates, conv_prev, conv_prev, conv_prev, s0,
      w_conv, w_conv, w_conv, a_row, dt_row, w_norm)


def _lru_kernel(x_ref, y_ref, prev_ref, h0_ref, wc_ref, bc_ref, wa_ref, ba_ref, wx_ref, bx_ref,
                lam_ref, wn_ref, o_ref, hl_ref, ext_ref, a_ref, b_ref, h_ref, *, rows, valid, groups):
    l = pl.program_id(1)
    first = l == 0

    @pl.when(first)
    def _():
        h_ref[...] = h0_ref[0]

    xc = _conv_block(ext_ref, x_ref[0], prev_ref, wc_ref, first, rows) + bc_ref[...]
    sub = lax.broadcasted_iota(jnp.int32, (rows // SUBLANES, SUBLANES, LRU_BLOCK), 1)
    for gi in range(groups):
        cs = slice(gi * LRU_BLOCK, (gi + 1) * LRU_BLOCK)
        xg = xc[:, cs]
        xg16 = xg.astype(BF16)
        gate_r = _sigmoid(_dot(xg16, wa_ref[gi]) + ba_ref[:, cs])
        gate_i = _sigmoid(_dot(xg16, wx_ref[gi]) + bx_ref[:, cs])
        log_a = -LRU_C * gate_r * _softplus(-lam_ref[:, cs])
        a = jnp.exp(log_a)
        var = 1.0 - jnp.exp(2.0 * log_a)
        bb = (var * lax.rsqrt(jnp.maximum(var, SQRT_FLOOR))) * gate_i * xg
        a = a.reshape(rows // SUBLANES, SUBLANES, LRU_BLOCK)
        bb = bb.reshape(rows // SUBLANES, SUBLANES, LRU_BLOCK)
        for s in (1, 2, 4):
            keep = sub >= s
            a_sh = jnp.where(keep, pltpu.roll(a, s, 1), 1.0)
            b_sh = jnp.where(keep, pltpu.roll(bb, s, 1), 0.0)
            bb = a * b_sh + bb
            a = a * a_sh
        a_ref[:, cs] = a.reshape(rows, LRU_BLOCK)
        b_ref[:, cs] = bb.reshape(rows, LRU_BLOCK)

    def step(i, h_prev):
        r0 = pl.multiple_of(i * SUBLANES, SUBLANES)
        hb = a_ref[pl.ds(r0, SUBLANES), :] * h_prev + b_ref[pl.ds(r0, SUBLANES), :]
        b_ref[pl.ds(r0, SUBLANES), :] = hb
        return hb[SUBLANES - 1:SUBLANES, :]

    h_ref[...] = lax.fori_loop(0, rows // SUBLANES, step, h_ref[...])

    hs = b_ref[...]
    last_blk, last_row = divmod(valid - 1, rows)

    @pl.when(l == last_blk)
    def _():
        hl_ref[0] = hs[last_row:last_row + 1, :]

    y = _gelu_tanh(y_ref[0]) * hs
    o_ref[0] = _rms_rows(y, wn_ref[...]).astype(o_ref.dtype)


def _lru_mixer(proj, conv_prev, h0, w_conv, b_conv, w_ga, b_ga, w_gx, b_gx, lam, w_norm,
               *, valid, rows, x_blk, y_blk):
    b, l, _ = proj.shape
    d = h0.shape[-1]
    groups = d // LRU_BLOCK
    nl = l // rows
    vec = pl.BlockSpec((1, d), lambda bi, li: (0, 0))
    gw = pl.BlockSpec((groups, LRU_BLOCK, LRU_BLOCK), lambda bi, li: (0, 0, 0))
    kern = functools.partial(_lru_kernel, rows=rows, valid=valid, groups=groups)
    return pl.pallas_call(
        kern,
        grid=(b, nl),
        in_specs=[pl.BlockSpec((1, rows, d), lambda bi, li: (bi, li, x_blk)),
                  pl.BlockSpec((1, rows, d), lambda bi, li: (bi, li, y_blk)),
                  pl.BlockSpec((1, CONV_W - 1, d), lambda bi, li: (bi, 0, 0)),
                  pl.BlockSpec((1, 1, d), lambda bi, li: (bi, 0, 0)),
                  pl.BlockSpec((CONV_W, d), lambda bi, li: (0, 0)),
                  vec, gw, vec, gw, vec, vec, vec],
        out_specs=[pl.BlockSpec((1, rows, d), lambda bi, li: (bi, li, 0)),
                   pl.BlockSpec((1, 1, d), lambda bi, li: (bi, 0, 0))],
        out_shape=[jax.ShapeDtypeStruct((b, l, d), BF16), jax.ShapeDtypeStruct((b, 1, d), F32)],
        scratch_shapes=[pltpu.VMEM((rows + SUBLANES, d), F32), pltpu.VMEM((rows, d), F32),
                        pltpu.VMEM((rows, d), F32), pltpu.VMEM((1, d), F32)],
        compiler_params=_cparams(("arbitrary", "arbitrary")),
        name="lru_mixer",
    )(proj, proj, conv_prev, h0, w_conv, b_conv, w_ga, b_ga, w_gx, b_gx, lam, w_norm)


def _out_proj_kernel(x_ref, a_ref, b_ref, wa_ref, wb_ref, o_ref):
    o_ref[...] = x_ref[...] + _dot(a_ref[...], wa_ref[...]) + _dot(b_ref[...], wb_ref[...])


def _out_proj(x, mix_a, mix_b, w_out, layer, tm, tn):
    t, d = x.shape
    da, db = mix_a.shape[1], mix_b.shape[1]
    assert da == db
    return pl.pallas_call(
        _out_proj_kernel,
        grid=(t // tm, d // tn),
        in_specs=[pl.BlockSpec((tm, tn), lambda i, j: (i, j)),
                  pl.BlockSpec((tm, da), lambda i, j: (i, 0)),
                  pl.BlockSpec((tm, db), lambda i, j: (i, 0)),
                  pl.BlockSpec((None, da, tn), lambda i, j: (layer, 0, j)),
                  pl.BlockSpec((None, db, tn), lambda i, j: (layer, 1, j))],
        out_specs=pl.BlockSpec((tm, tn), lambda i, j: (i, j)),
        out_shape=jax.ShapeDtypeStruct((t, d), F32),
        compiler_params=_cparams(("arbitrary", "arbitrary")),
        name="out_proj",
    )(x, mix_a, mix_b, w_out, w_out)


def _mlp_up_kernel(x_ref, wn_ref, wu_ref, h_ref, xn_ref):
    @pl.when(pl.program_id(1) == 0)
    def _():
        xn_ref[...] = _rms_rows(x_ref[...], wn_ref[...]).astype(BF16)

    hid = jnp.maximum(_dot(xn_ref[...], wu_ref[...]), 0.0)
    h_ref[...] = (hid * hid).astype(BF16)


def _mlp_up(x, wn, w_up, layer, tm, tn):
    t, d = x.shape
    f = w_up.shape[2]
    return pl.pallas_call(
        _mlp_up_kernel,
        grid=(t // tm, f // tn),
        in_specs=[pl.BlockSpec((tm, d), lambda i, j: (i, 0)),
                  pl.BlockSpec((1, d), lambda i, j: (0, 0)),
                  pl.BlockSpec((None, d, tn), lambda i, j: (layer, 0, j))],
        out_specs=pl.BlockSpec((tm, tn), lambda i, j: (i, j)),
        out_shape=jax.ShapeDtypeStruct((t, f), BF16),
        scratch_shapes=[pltpu.VMEM((tm, d), BF16)],
        compiler_params=_cparams(("arbitrary", "arbitrary")),
        name="mlp_up",
    )(x, wn, w_up)


def _mlp_down_kernel(x_ref, h_ref, wd_ref, o_ref):
    @pl.when(pl.program_id(2) == 0)
    def _():
        o_ref[...] = x_ref[...]

    o_ref[...] += _dot(h_ref[...], wd_ref[...])


def _mlp_down(x, hid, w_down, layer, tm, tn, tk):
    t, d = x.shape
    f = hid.shape[1]
    return pl.pallas_call(
        _mlp_down_kernel,
        grid=(t // tm, d // tn, f // tk),
        in_specs=[pl.BlockSpec((tm, tn), lambda i, j, k: (i, j)),
                  pl.BlockSpec((tm, tk), lambda i, j, k: (i, k)),
                  pl.BlockSpec((None, tk, tn), lambda i, j, k: (layer, k, j))],
        out_specs=pl.BlockSpec((tm, tn), lambda i, j, k: (i, j)),
        out_shape=jax.ShapeDtypeStruct((t, d), F32),
        compiler_params=_cparams(("arbitrary", "arbitrary", "arbitrary")),
        name="mlp_down",
    )(x, hid, w_down)


def _final_norm_kernel(x_ref, w_ref, o_ref):
    o_ref[...] = _rms_rows(x_ref[...], w_ref[...])


def _final_norm(x, w, tm):
    t, d = x.shape
    return pl.pallas_call(
        _final_norm_kernel,
        grid=(t // tm,),
        in_specs=[pl.BlockSpec((tm, d), lambda i: (i, 0)), pl.BlockSpec((1, d), lambda i: (0, 0))],
        out_specs=pl.BlockSpec((tm, d), lambda i: (i, 0)),
        out_shape=jax.ShapeDtypeStruct((t, d), F32),
        compiler_params=_cparams(("arbitrary",)),
        name="final_norm",
    )(x, w)


def _tile(n, pref):
    t = min(n, pref)
    while n % t:
        t //= 2
    return t


def _prep_stacked(p):
    d_a = p["w_conv_delta"].shape[-1] // 3
    heads = p["a_log"].shape[-1]
    w_in = p["w_in"]
    gate_lo, gate_hi = 4 * d_a, 4 * d_a + 2 * heads
    return dict(
        w_main=jnp.concatenate([w_in[:, :, :gate_lo], w_in[:, :, gate_hi:]], axis=2).astype(BF16),
        w_gate=jnp.pad(w_in[:, :, gate_lo:gate_hi], ((0, 0), (0, 0), (0, LANES - 2 * heads))).astype(BF16),
        w_out=p["w_out"].astype(BF16), w_mlp_up=p["w_mlp_up"].astype(BF16),
        w_mlp_down=p["w_mlp_down"].astype(BF16))


def _prep_layer(i, p, stacked):
    heads = p["a_log"].shape[-1]
    row = lambda v: jnp.pad(v, (heads, LANES - 2 * heads))[None, :].astype(F32)
    return dict(
        stacked, layer=i,
        w_norm_mix=p["w_norm_mix"][i][None, :],
        w_conv_delta=p["w_conv_delta"][i], a_row=row(p["a_log"][i]), dt_row=row(p["dt_bias"][i]),
        w_norm_delta=p["w_norm_delta"][i][None, :],
        w_conv_lru=p["w_conv_lru"][i], b_conv_lru=p["b_conv_lru"][i][None, :],
        w_gate_a=p["w_gate_a"][i].astype(BF16), b_gate_a=p["b_gate_a"][i][None, :],
        w_gate_x=p["w_gate_x"][i].astype(BF16), b_gate_x=p["b_gate_x"][i][None, :],
        lam=p["lam"][i][None, :], w_norm_lru=p["w_norm_lru"][i][None, :],
        w_norm_mlp=p["w_norm_mlp"][i][None, :],
    )


def _layer(x, conv_d, s_d, conv_l, h_l, p, w_final, final_norm):
    b, l, d = x.shape
    t = b * l
    d_a = p["w_conv_delta"].shape[-1] // 3
    d_b = p["w_conv_lru"].shape[-1]
    assert d_a == d_b and d_a % HEAD_DIM == 0
    layer = p["layer"]
    tm = _tile(t, 512)
    xf = x.reshape(t, d)

    proj, gates = _norm_proj(xf, p["w_norm_mix"], p["w_main"], p["w_gate"], layer, tm,
                             _tile(p["w_main"].shape[2], 1024))
    n_main = proj.shape[1]
    proj = proj.reshape(b, l, n_main)
    conv_d_new = jnp.concatenate([conv_d, proj[:, :, :3 * d_a]], axis=1)[:, -(CONV_W - 1):]
    conv_l_new = jnp.concatenate([conv_l, proj[:, :, 4 * d_a:4 * d_a + d_b]], axis=1)[:, -(CONV_W - 1):]

    lp = -(-l // CHUNK) * CHUNK
    if lp != l:
        proj = jnp.pad(proj, ((0, 0), (0, lp - l), (0, 0)))
        gates = jnp.pad(gates.reshape(b, l, LANES), ((0, 0), (0, lp - l), (0, 0))).reshape(b * lp, LANES)

    delta_out, s_d_new = _delta_mixer(
        proj, gates, conv_d, s_d, p["w_conv_delta"], p["a_row"], p["dt_row"], p["w_norm_delta"],
        valid=l, rows=_tile(lp, 2048))
    lru_out, h_new = _lru_mixer(
        proj, conv_l, h_l[:, None, :], p["w_conv_lru"], p["b_conv_lru"], p["w_gate_a"], p["b_gate_a"],
        p["w_gate_x"], p["b_gate_x"], p["lam"], p["w_norm_lru"],
        valid=l, rows=_tile(lp, 256), x_blk=4 * d_a // d_b, y_blk=4 * d_a // d_b + 1)
    if lp != l:
        delta_out, lru_out = delta_out[:, :l], lru_out[:, :l]

    x1 = _out_proj(xf, delta_out.reshape(t, d_a), lru_out.reshape(t, d_b), p["w_out"], layer,
                   _tile(t, 1024), _tile(d, 1024))
    f = p["w_mlp_up"].shape[2]
    hid = _mlp_up(x1, p["w_norm_mlp"], p["w_mlp_up"], layer, tm, _tile(f, 1024))
    x2 = _mlp_down(x1, hid, p["w_mlp_down"], layer, _tile(t, 1024), _tile(d, 1024), _tile(f, 4096))
    if final_norm:
        x2 = _final_norm(x2, w_final, tm)
    return x2.reshape(b, l, d), s_d_new, conv_d_new, h_new[:, 0, :], conv_l_new


def _trunk(x, s_d, conv_d, h_l, conv_l, layers, w_final):
    sds, cds, hls, cls = [], [], [], []
    depth = len(layers)
    for i, p in enumerate(layers):
        x, sd, cd, hl, cl = _layer(x, conv_d[i], s_d[i], conv_l[i], h_l[i], p, w_final, i == depth - 1)
        sds.append(sd)
        cds.append(cd)
        hls.append(hl)
        cls.append(cl)
    return x, jnp.stack(sds), jnp.stack(cds), jnp.stack(hls), jnp.stack(cls)


def kernel(x_prompt, x_sample, state_delta, state_conv_delta, state_lru, state_conv_lru,
           w_norm_mix, w_in, w_conv_delta, a_log, dt_bias, w_norm_delta,
           w_conv_lru, b_conv_lru, w_gate_a, b_gate_a, w_gate_x, b_gate_x,
           lam, w_norm_lru, w_out, w_norm_mlp, w_mlp_up, w_mlp_down, w_norm_final):
    params = dict(w_norm_mix=w_norm_mix, w_in=w_in, w_conv_delta=w_conv_delta, a_log=a_log, dt_bias=dt_bias,
                  w_norm_delta=w_norm_delta, w_conv_lru=w_conv_lru, b_conv_lru=b_conv_lru,
                  w_gate_a=w_gate_a, b_gate_a=b_gate_a, w_gate_x=w_gate_x, b_gate_x=b_gate_x,
                  lam=lam, w_norm_lru=w_norm_lru, w_out=w_out, w_norm_mlp=w_norm_mlp,
                  w_mlp_up=w_mlp_up, w_mlp_down=w_mlp_down)
    depth = w_in.shape[0]
    stacked = _prep_stacked(params)
    layers = [_prep_layer(i, params, stacked) for i in range(depth)]
    w_final = w_norm_final[None, :]

    bp = x_prompt.shape[0]
    z_sd = jnp.zeros((depth, bp) + state_delta.shape[2:], F32)
    z_cd = jnp.zeros((depth, bp) + state_conv_delta.shape[2:], x_prompt.dtype)
    z_hl = jnp.zeros((depth, bp) + state_lru.shape[2:], F32)
    z_cl = jnp.zeros((depth, bp) + state_conv_lru.shape[2:], x_prompt.dtype)
    outs_p = _trunk(x_prompt, z_sd, z_cd, z_hl, z_cl, layers, w_final)
    outs_s = _trunk(x_sample, state_delta, state_conv_delta, state_lru, state_conv_lru, layers, w_final)
    return (outs_p[0], outs_s[0]) + outs_p[1:] + outs_s[1:]
```

```python
import functools

import jax
import jax.numpy as jnp
from jax import lax
from jax.experimental import pallas as pl
from jax.experimental.pallas import tpu as pltpu

HEAD_DIM = 128
LRU_BLOCK = 128
CONV_W = 4
LRU_C = 8.0
EPS = 1e-6
LANES = 128
SUBLANES = 8
CHUNK = 128
NILPOTENT_BLOCK = 16
GROUP_CHUNKS = 4
DELTA_UNITS = 16
SQRT_FLOOR = 1e-30
VMEM_LIMIT = 56 * 1024 * 1024

F32 = jnp.float32
BF16 = jnp.bfloat16


def _cparams(sem):
    return pltpu.CompilerParams(dimension_semantics=sem, vmem_limit_bytes=VMEM_LIMIT)


def _dot(a, b):
    return jnp.dot(a, b, preferred_element_type=F32)


def _dot_nt(a, b):
    return lax.dot_general(a, b, (((1,), (1,)), ((), ())), preferred_element_type=F32)


def _split_bf16(x):
    hi = x.astype(BF16)
    lo = (x - hi.astype(F32)).astype(BF16)
    return hi, lo


def _dot3(xs, ys):
    (xh, xl), (yh, yl) = xs, ys
    lhs = jnp.concatenate([xh, xl, xh], axis=1)
    rhs = jnp.concatenate([yh, yh, yl], axis=0)
    return _dot(lhs, rhs)


def _cumsum_rows(tril16, x):
    hi, lo = _split_bf16(x)
    lo2 = (x - hi.astype(F32) - lo.astype(F32)).astype(BF16)
    lhs = jnp.concatenate([tril16, tril16, tril16], axis=1)
    rhs = jnp.concatenate([hi, lo, lo2], axis=0)
    return _dot(lhs, rhs)


def _sigmoid(x):
    return 0.5 * jnp.tanh(0.5 * x) + 0.5


def _softplus(x):
    return jnp.maximum(x, 0.0) + jnp.log1p(jnp.exp(-jnp.abs(x)))


def _silu(x):
    return x * _sigmoid(x)


def _gelu_tanh(x):
    c = 0.7978845608028654
    return 0.5 * x * (1.0 + jnp.tanh(c * (x + 0.044715 * (x * x * x))))


def _rms_rows(x, w):
    ms = jnp.mean(x * x, axis=-1, keepdims=True)
    return (x * lax.rsqrt(ms + EPS)) * w


def _norm_proj_kernel(x_ref, wn_ref, w_ref, wg_ref, o_ref, g_ref, xn_ref):
    @pl.when(pl.program_id(1) == 0)
    def _():
        xn_ref[...] = _rms_rows(x_ref[...], wn_ref[...]).astype(BF16)
        g_ref[...] = _dot(xn_ref[...], wg_ref[...])

    o_ref[...] = _dot(xn_ref[...], w_ref[...])


def _norm_proj(x, wn, w_main, w_gate, layer, tm, tn):
    t, d = x.shape
    n = w_main.shape[2]
    return pl.pallas_call(
        _norm_proj_kernel,
        grid=(t // tm, n // tn),
        in_specs=[
            pl.BlockSpec((tm, d), lambda i, j: (i, 0)),
            pl.BlockSpec((1, d), lambda i, j: (0, 0)),
            pl.BlockSpec((None, d, tn), lambda i, j: (layer, 0, j)),
            pl.BlockSpec((None, d, LANES), lambda i, j: (layer, 0, 0)),
        ],
        out_specs=[
            pl.BlockSpec((tm, tn), lambda i, j: (i, j)),
            pl.BlockSpec((tm, LANES), lambda i, j: (i, 0)),
        ],
        out_shape=[jax.ShapeDtypeStruct((t, n), F32), jax.ShapeDtypeStruct((t, LANES), F32)],
        scratch_shapes=[pltpu.VMEM((tm, d), BF16)],
        compiler_params=_cparams(("arbitrary", "arbitrary")),
        name="norm_proj",
    )(x, wn, w_main, w_gate)


def _conv_block(ext_ref, x, prev_ref, w_ref, first, rows):
    lo = SUBLANES - (CONV_W - 1)

    @pl.when(first)
    def _():
        ext_ref[lo:SUBLANES, :] = prev_ref[0]

    @pl.when(jnp.logical_not(first))
    def _():
        ext_ref[lo:SUBLANES, :] = ext_ref[rows + lo:rows + SUBLANES, :]

    ext_ref[SUBLANES:SUBLANES + rows, :] = x
    y = ext_ref[lo:lo + rows, :] * w_ref[0:1, :]
    for j in range(1, CONV_W):
        y = y + ext_ref[lo + j:lo + j + rows, :] * w_ref[j:j + 1, :]
    return y


def _unit_lower_inverse_stages(mats, row, col, live, out):
    n = min(mats[0].shape[0], live)
    eye = (row == col).astype(F32)
    shift = NILPOTENT_BLOCK.bit_length() - 1
    same = (row >> shift) == (col >> shift)
    ps = [jnp.where(same, a, 0.0) for a in mats]
    ts = [eye - p for p in ps]
    pss = [_split_bf16(p) for p in ps]
    size = 2
    while size < NILPOTENT_BLOCK:
        pss = [_split_bf16(_dot3(p, p)) for p in pss]
        yield
        ts = [t + _dot3(_split_bf16(t), p) for t, p in zip(ts, pss)]
        yield
        size *= 2
    blk = NILPOTENT_BLOCK
    tss = [_split_bf16(t) for t in ts]
    while blk < n:
        shift += 1
        nxt = (row >> shift) == (col >> shift)
        sel = jnp.logical_and(nxt, jnp.logical_not(same))
        nss = [_split_bf16(_dot3(t, _split_bf16(jnp.where(sel, a, 0.0)))) for t, a in zip(tss, mats)]
        yield
        ts = [t - _dot3(nm, t2) for t, nm, t2 in zip(ts, nss, tss)]
        tss = [_split_bf16(t) for t in ts]
        yield
        same = nxt
        blk *= 2
    out.extend(tss)


def _interleave(main, side, every):
    for count, _ in enumerate(main, 1):
        if count % every == 0:
            next(side, None)
    for _ in side:
        pass


def _delta_kernel(q_ref, k_ref, v_ref, z_ref, g_ref, pq_ref, pk_ref, pv_ref, s0_ref,
                  wq_ref, wk_ref, wv_ref, arow_ref, dtrow_ref, wnorm_ref,
                  o_ref, s_ref, eq_ref, ek_ref, ev_ref, st_ref, *, rows, nblk, valid, heads, hpb):
    head0 = pl.program_id(1) * hpb
    l = pl.program_id(2)
    first = l == 0

    @pl.when(first)
    def _():
        st_ref[...] = s0_ref[0]

    q_all = _silu(_conv_block(eq_ref, q_ref[0], pq_ref, wq_ref, first, rows))
    k_all = _silu(_conv_block(ek_ref, k_ref[0], pk_ref, wk_ref, first, rows))
    v_all = _silu(_conv_block(ev_ref, v_ref[0], pv_ref, wv_ref, first, rows))
    zg = _silu(z_ref[0])

    gates = g_ref[...]
    lane = lax.broadcasted_iota(jnp.int32, gates.shape, 1)
    beta_all = _sigmoid(gates)
    g_all = -jnp.exp(arow_ref[...]) * _softplus(gates + dtrow_ref[...])
    if valid < rows * nblk:
        live_rows = l * rows + lax.broadcasted_iota(jnp.int32, (rows, 1), 0) < valid
    per_head = []
    for j in range(hpb):
        hs = slice(j * HEAD_DIM, (j + 1) * HEAD_DIM)
        q, k = q_all[:, hs], k_all[:, hs]
        q = q * lax.rsqrt(jnp.sum(q * q, axis=-1, keepdims=True) + EPS) * (HEAD_DIM ** -0.5)
        k = k * lax.rsqrt(jnp.sum(k * k, axis=-1, keepdims=True) + EPS)
        beta = jnp.sum(jnp.where(lane == head0 + j, beta_all, 0.0), axis=1, keepdims=True)
        g = jnp.sum(jnp.where(lane == heads + head0 + j, g_all, 0.0), axis=1, keepdims=True)
        if valid < rows * nblk:
            beta = jnp.where(live_rows, beta, 0.0)
            g = jnp.where(live_rows, g, 0.0)
            k = jnp.where(live_rows, k, 0.0)
        per_head.append((q, k, v_all[:, hs], beta, g))

    row = lax.broadcasted_iota(jnp.int32, (CHUNK, CHUNK), 0)
    col = lax.broadcasted_iota(jnp.int32, (CHUNK, CHUNK), 1)
    causal = row >= col
    strict = row > col
    tril16 = causal.astype(BF16)

    live = min(CHUNK, valid) if nblk * rows == CHUNK else CHUNK

    def prepare(units, out):
        pre = []
        for j, c in units:
            q, k, v, beta, g = per_head[j]
            sl = slice(c * CHUNK, (c + 1) * CHUNK)
            qc, kc, vc = q[sl], k[sl], v[sl]
            beta_b = jnp.broadcast_to(beta[sl], (CHUNK, CHUNK))
            g_b = jnp.broadcast_to(g[sl], (CHUNK, CHUNK))
            gcum = _cumsum_rows(tril16, g_b)
            decay = jnp.where(causal, jnp.exp(jnp.where(causal, gcum - gcum.T, 0.0)), 0.0)
            eg = jnp.exp(gcum)
            kb = kc * beta_b
            kc16 = kc.astype(BF16)
            a_mat = jnp.where(strict, _dot_nt(kb.astype(BF16), kc16) * decay, 0.0)
            g_last = gcum[CHUNK - 1:CHUNK, :]
            pre.append(dict(
                a=a_mat, rhs=jnp.concatenate([vc * beta_b, kb * eg], axis=1),
                qk=(_dot_nt(qc.astype(BF16), kc16) * decay).astype(BF16), qg=qc * eg,
                kdt=(kc * jnp.exp(g_last - gcum)).T.astype(BF16), gl=jnp.exp(g_last)))
            yield
        t_invs = []
        yield from _unit_lower_inverse_stages([p["a"] for p in pre], row, col, live, t_invs)
        sols = [_dot3(t, _split_bf16(p["rhs"])).astype(BF16) for t, p in zip(t_invs, pre)]
        yield
        for unit, p, sol in zip(units, pre, sols):
            ks = _dot(p["kdt"], sol)
            qs = _dot(p["qk"], sol)
            lhs = jnp.concatenate([p["qg"] - qs[:, HEAD_DIM:], ks[:, HEAD_DIM:]], axis=0).astype(BF16)
            out[unit] = (lhs, qs[:, :HEAD_DIM], ks[:, :HEAD_DIM], p["gl"])
        yield

    state = [st_ref[j] for j in range(hpb)]

    def recur(units, ops):
        for j, c in units:
            sl = slice(c * CHUNK, (c + 1) * CHUNK)
            hs = slice(j * HEAD_DIM, (j + 1) * HEAD_DIM)
            lhs, o0, s_add, gl = ops[(j, c)]
            s = state[j]
            both = _dot(lhs, s.astype(BF16))
            o = both[:CHUNK] + o0
            state[j] = (s * gl - both[CHUNK:]) + s_add
            o_ref[0, sl, hs] = (_rms_rows(o, wnorm_ref[...]) * zg[sl, hs]).astype(o_ref.dtype)
            yield

    units = [(j, c) for j in range(hpb) for c in range(rows // CHUNK)]
    n = len(units)
    sizes = [n] if n <= GROUP_CHUNKS else [n // 2] + [GROUP_CHUNKS] * (n // 2 // GROUP_CHUNKS)
    n_products = 2 * (NILPOTENT_BLOCK.bit_length() - 2) + 2 * ((CHUNK // NILPOTENT_BLOCK).bit_length() - 1)
    ops = {}
    pending, n_pending, start = iter(()), 1, 0
    for size in sizes:
        group = units[start:start + size]
        start += size
        every = max(1, (size + n_products + 2) // n_pending)
        _interleave(prepare(group, ops), pending, every)
        pending, n_pending = recur(group, ops), size
    for _ in pending:
        pass
    for j in range(hpb):
        st_ref[j] = state[j]

    @pl.when(l == nblk - 1)
    def _():
        s_ref[0] = st_ref[...]


def _delta_mixer(proj, gates, conv_prev, s0, w_conv, a_row, dt_row, w_norm, *, valid, rows, hpb):
    b, l, _ = proj.shape
    heads = s0.shape[1]
    nl = l // rows
    ng = heads // hpb
    width = hpb * HEAD_DIM
    blk = lambda off: pl.BlockSpec((1, rows, width), lambda bi, gi, li, off=off: (bi, li, off * ng + gi))
    prev = lambda off: pl.BlockSpec((1, CONV_W - 1, width), lambda bi, gi, li, off=off: (bi, 0, off * ng + gi))
    wcv = lambda off: pl.BlockSpec((CONV_W, width), lambda bi, gi, li, off=off: (0, off * ng + gi))
    row_spec = pl.BlockSpec((1, LANES), lambda bi, gi, li: (0, 0))
    state_spec = pl.BlockSpec((1, hpb, HEAD_DIM, HEAD_DIM), lambda bi, gi, li: (bi, gi, 0, 0))
    kern = functools.partial(_delta_kernel, rows=rows, nblk=nl, valid=valid, heads=heads, hpb=hpb)
    return pl.pallas_call(
        kern,
        grid=(b, ng, nl),
        in_specs=[blk(0), blk(1), blk(2), blk(3),
                  pl.BlockSpec((rows, LANES), lambda bi, gi, li: (bi * nl + li, 0)),
                  prev(0), prev(1), prev(2), state_spec,
                  wcv(0), wcv(1), wcv(2), row_spec, row_spec, row_spec],
        out_specs=[pl.BlockSpec((1, rows, width), lambda bi, gi, li: (bi, li, gi)), state_spec],
        out_shape=[jax.ShapeDtypeStruct((b, l, heads * HEAD_DIM), BF16),
                   jax.ShapeDtypeStruct(s0.shape, F32)],
        scratch_shapes=[pltpu.VMEM((rows + SUBLANES, width), F32)] * 3
        + [pltpu.VMEM((hpb, HEAD_DIM, HEAD_DIM), F32)],
        compiler_params=_cparams(("arbitrary", "arbitrary", "arbitrary")),
        name="delta_mixer",
    )(proj, proj, proj, proj, gates, conv_prev, conv_prev, conv_prev, s0,
      w_conv, w_conv, w_conv, a_row, dt_row, w_norm)


def _lru_kernel(x_ref, y_ref, prev_ref, h0_ref, wc_ref, bc_ref, wa_ref, ba_ref, wx_ref, bx_ref,
                lam_ref, wn_ref, o_ref, hl_ref, ext_ref, a_ref, b_ref, h_ref, *, rows, valid, groups):
    l = pl.program_id(1)
    first = l == 0

    @pl.when(first)
    def _():
        h_ref[...] = h0_ref[0]

    xc = _conv_block(ext_ref, x_ref[0], prev_ref, wc_ref, first, rows) + bc_ref[...]
    sub = lax.broadcasted_iota(jnp.int32, (rows // SUBLANES, SUBLANES, LRU_BLOCK), 1)
    for gi in range(groups):
        cs = slice(gi * LRU_BLOCK, (gi + 1) * LRU_BLOCK)
        xg = xc[:, cs]
        xg16 = xg.astype(BF16)
        gate_r = _sigmoid(_dot(xg16, wa_ref[gi]) + ba_ref[:, cs])
        gate_i = _sigmoid(_dot(xg16, wx_ref[gi]) + bx_ref[:, cs])
        log_a = -LRU_C * gate_r * _softplus(-lam_ref[:, cs])
        a = jnp.exp(log_a)
        var = 1.0 - jnp.exp(2.0 * log_a)
        bb = (var * lax.rsqrt(jnp.maximum(var, SQRT_FLOOR))) * gate_i * xg
        a = a.reshape(rows // SUBLANES, SUBLANES, LRU_BLOCK)
        bb = bb.reshape(rows // SUBLANES, SUBLANES, LRU_BLOCK)
        for s in (1, 2, 4):
            keep = sub >= s
            a_sh = jnp.where(keep, pltpu.roll(a, s, 1), 1.0)
            b_sh = jnp.where(keep, pltpu.roll(bb, s, 1), 0.0)
            bb = a * b_sh + bb
            a = a * a_sh
        a_ref[:, cs] = a.reshape(rows, LRU_BLOCK)
        b_ref[:, cs] = bb.reshape(rows, LRU_BLOCK)

    def step(i, h_prev):
        r0 = pl.multiple_of(i * SUBLANES, SUBLANES)
        hb = a_ref[pl.ds(r0, SUBLANES), :] * h_prev + b_ref[pl.ds(r0, SUBLANES), :]
        b_ref[pl.ds(r0, SUBLANES), :] = hb
        return hb[SUBLANES - 1:SUBLANES, :]

    h_ref[...] = lax.fori_loop(0, rows // SUBLANES, step, h_ref[...])

    hs = b_ref[...]
    last_blk, last_row = divmod(valid - 1, rows)

    @pl.when(l == last_blk)
    def _():
        hl_ref[0] = hs[last_row:last_row + 1, :]

    y = _gelu_tanh(y_ref[0]) * hs
    o_ref[0] = _rms_rows(y, wn_ref[...]).astype(o_ref.dtype)


def _lru_mixer(proj, conv_prev, h0, w_conv, b_conv, w_ga, b_ga, w_gx, b_gx, lam, w_norm,
               *, valid, rows, x_blk, y_blk):
    b, l, _ = proj.shape
    d = h0.shape[-1]
    groups = d // LRU_BLOCK
    nl = l // rows
    vec = pl.BlockSpec((1, d), lambda bi, li: (0, 0))
    gw = pl.BlockSpec((groups, LRU_BLOCK, LRU_BLOCK), lambda bi, li: (0, 0, 0))
    kern = functools.partial(_lru_kernel, rows=rows, valid=valid, groups=groups)
    return pl.pallas_call(
        kern,
        grid=(b, nl),
        in_specs=[pl.BlockSpec((1, rows, d), lambda bi, li: (bi, li, x_blk)),
                  pl.BlockSpec((1, rows, d), lambda bi, li: (bi, li, y_blk)),
                  pl.BlockSpec((1, CONV_W - 1, d), lambda bi, li: (bi, 0, 0)),
                  pl.BlockSpec((1, 1, d), lambda bi, li: (bi, 0, 0)),
                  pl.BlockSpec((CONV_W, d), lambda bi, li: (0, 0)),
                  vec, gw, vec, gw, vec, vec, vec],
        out_specs=[pl.BlockSpec((1, rows, d), lambda bi, li: (bi, li, 0)),
                   pl.BlockSpec((1, 1, d), lambda bi, li: (bi, 0, 0))],
        out_shape=[jax.ShapeDtypeStruct((b, l, d), BF16), jax.ShapeDtypeStruct((b, 1, d), F32)],
        scratch_shapes=[pltpu.VMEM((rows + SUBLANES, d), F32), pltpu.VMEM((rows, d), F32),
                        pltpu.VMEM((rows, d), F32), pltpu.VMEM((1, d), F32)],
        compiler_params=_cparams(("arbitrary", "arbitrary")),
        name="lru_mixer",
    )(proj, proj, conv_prev, h0, w_conv, b_conv, w_ga, b_ga, w_gx, b_gx, lam, w_norm)


def _out_proj_kernel(x_ref, a_ref, b_ref, wa_ref, wb_ref, o_ref):
    o_ref[...] = x_ref[...] + _dot(a_ref[...], wa_ref[...]) + _dot(b_ref[...], wb_ref[...])


def _out_proj(x, mix_a, mix_b, w_out, layer, tm, tn):
    t, d = x.shape
    da, db = mix_a.shape[1], mix_b.shape[1]
    assert da == db
    return pl.pallas_call(
        _out_proj_kernel,
        grid=(t // tm, d // tn),
        in_specs=[pl.BlockSpec((tm, tn), lambda i, j: (i, j)),
                  pl.BlockSpec((tm, da), lambda i, j: (i, 0)),
                  pl.BlockSpec((tm, db), lambda i, j: (i, 0)),
                  pl.BlockSpec((None, da, tn), lambda i, j: (layer, 0, j)),
                  pl.BlockSpec((None, db, tn), lambda i, j: (layer, 1, j))],
        out_specs=pl.BlockSpec((tm, tn), lambda i, j: (i, j)),
        out_shape=jax.ShapeDtypeStruct((t, d), F32),
        compiler_params=_cparams(("arbitrary", "arbitrary")),
        name="out_proj",
    )(x, mix_a, mix_b, w_out, w_out)


def _mlp_up_kernel(x_ref, wn_ref, wu_ref, h_ref, xn_ref):
    @pl.when(pl.program_id(1) == 0)
    def _():
        xn_ref[...] = _rms_rows(x_ref[...], wn_ref[...]).astype(BF16)

    hid = jnp.maximum(_dot(xn_ref[...], wu_ref[...]), 0.0)
    h_ref[...] = (hid * hid).astype(BF16)


def _mlp_up(x, wn, w_up, layer, tm, tn):
    t, d = x.shape
    f = w_up.shape[2]
    return pl.pallas_call(
        _mlp_up_kernel,
        grid=(t // tm, f // tn),
        in_specs=[pl.BlockSpec((tm, d), lambda i, j: (i, 0)),
                  pl.BlockSpec((1, d), lambda i, j: (0, 0)),
                  pl.BlockSpec((None, d, tn), lambda i, j: (layer, 0, j))],
        out_specs=pl.BlockSpec((tm, tn), lambda i, j: (i, j)),
        out_shape=jax.ShapeDtypeStruct((t, f), BF16),
        scratch_shapes=[pltpu.VMEM((tm, d), BF16)],
        compiler_params=_cparams(("arbitrary", "arbitrary")),
        name="mlp_up",
    )(x, wn, w_up)


def _mlp_down_kernel(x_ref, h_ref, wd_ref, o_ref):
    @pl.when(pl.program_id(2) == 0)
    def _():
        o_ref[...] = x_ref[...]

    o_ref[...] += _dot(h_ref[...], wd_ref[...])


def _mlp_down(x, hid, w_down, layer, tm, tn, tk):
    t, d = x.shape
    f = hid.shape[1]
    return pl.pallas_call(
        _mlp_down_kernel,
        grid=(t // tm, d // tn, f // tk),
        in_specs=[pl.BlockSpec((tm, tn), lambda i, j, k: (i, j)),
                  pl.BlockSpec((tm, tk), lambda i, j, k: (i, k)),
                  pl.BlockSpec((None, tk, tn), lambda i, j, k: (layer, k, j))],
        out_specs=pl.BlockSpec((tm, tn), lambda i, j, k: (i, j)),
        out_shape=jax.ShapeDtypeStruct((t, d), F32),
        compiler_params=_cparams(("arbitrary", "arbitrary", "arbitrary")),
        name="mlp_down",
    )(x, hid, w_down)


def _final_norm_kernel(x_ref, w_ref, o_ref):
    o_ref[...] = _rms_rows(x_ref[...], w_ref[...])


def _final_norm(x, w, tm):
    t, d = x.shape
    return pl.pallas_call(
        _final_norm_kernel,
        grid=(t // tm,),
        in_specs=[pl.BlockSpec((tm, d), lambda i: (i, 0)), pl.BlockSpec((1, d), lambda i: (0, 0))],
        out_specs=pl.BlockSpec((tm, d), lambda i: (i, 0)),
        out_shape=jax.ShapeDtypeStruct((t, d), F32),
        compiler_params=_cparams(("arbitrary",)),
        name="final_norm",
    )(x, w)


def _tile(n, pref):
    t = min(n, pref)
    while n % t:
        t //= 2
    return t


def _prep_stacked(p):
    d_a = p["w_conv_delta"].shape[-1] // 3
    heads = p["a_log"].shape[-1]
    w_in = p["w_in"]
    gate_lo, gate_hi = 4 * d_a, 4 * d_a + 2 * heads
    return dict(
        w_main=jnp.concatenate([w_in[:, :, :gate_lo], w_in[:, :, gate_hi:]], axis=2).astype(BF16),
        w_gate=jnp.pad(w_in[:, :, gate_lo:gate_hi], ((0, 0), (0, 0), (0, LANES - 2 * heads))).astype(BF16),
        w_out=p["w_out"].astype(BF16), w_mlp_up=p["w_mlp_up"].astype(BF16),
        w_mlp_down=p["w_mlp_down"].astype(BF16))


def _prep_layer(i, p, stacked):
    heads = p["a_log"].shape[-1]
    row = lambda v: jnp.pad(v, (heads, LANES - 2 * heads))[None, :].astype(F32)
    return dict(
        stacked, layer=i,
        w_norm_mix=p["w_norm_mix"][i][None, :],
        w_conv_delta=p["w_conv_delta"][i], a_row=row(p["a_log"][i]), dt_row=row(p["dt_bias"][i]),
        w_norm_delta=p["w_norm_delta"][i][None, :],
        w_conv_lru=p["w_conv_lru"][i], b_conv_lru=p["b_conv_lru"][i][None, :],
        w_gate_a=p["w_gate_a"][i].astype(BF16), b_gate_a=p["b_gate_a"][i][None, :],
        w_gate_x=p["w_gate_x"][i].astype(BF16), b_gate_x=p["b_gate_x"][i][None, :],
        lam=p["lam"][i][None, :], w_norm_lru=p["w_norm_lru"][i][None, :],
        w_norm_mlp=p["w_norm_mlp"][i][None, :],
    )


def _layer(x, conv_d, s_d, conv_l, h_l, p, w_final, final_norm):
    b, l, d = x.shape
    t = b * l
    d_a = p["w_conv_delta"].shape[-1] // 3
    d_b = p["w_conv_lru"].shape[-1]
    assert d_a == d_b and d_a % HEAD_DIM == 0
    layer = p["layer"]
    tm = _tile(t, 512)
    xf = x.reshape(t, d)

    proj, gates = _norm_proj(xf, p["w_norm_mix"], p["w_main"], p["w_gate"], layer, tm,
                             _tile(p["w_main"].shape[2], 1024))
    n_main = proj.shape[1]
    proj = proj.reshape(b, l, n_main)
    conv_d_new = jnp.concatenate([conv_d, proj[:, :, :3 * d_a]], axis=1)[:, -(CONV_W - 1):]
    conv_l_new = jnp.concatenate([conv_l, proj[:, :, 4 * d_a:4 * d_a + d_b]], axis=1)[:, -(CONV_W - 1):]

    lp = -(-l // CHUNK) * CHUNK
    if lp != l:
        proj = jnp.pad(proj, ((0, 0), (0, lp - l), (0, 0)))
        gates = jnp.pad(gates.reshape(b, l, LANES), ((0, 0), (0, lp - l), (0, 0))).reshape(b * lp, LANES)

    delta_rows = _tile(lp, DELTA_UNITS * CHUNK)
    delta_out, s_d_new = _delta_mixer(
        proj, gates, conv_d, s_d, p["w_conv_delta"], p["a_row"], p["dt_row"], p["w_norm_delta"],
        valid=l, rows=delta_rows, hpb=_tile(d_a // HEAD_DIM, DELTA_UNITS * CHUNK // delta_rows))
    lru_out, h_new = _lru_mixer(
        proj, conv_l, h_l[:, None, :], p["w_conv_lru"], p["b_conv_lru"], p["w_gate_a"], p["b_gate_a"],
        p["w_gate_x"], p["b_gate_x"], p["lam"], p["w_norm_lru"],
        valid=l, rows=_tile(lp, 256), x_blk=4 * d_a // d_b, y_blk=4 * d_a // d_b + 1)
    if lp != l:
        delta_out, lru_out = delta_out[:, :l], lru_out[:, :l]

    x1 = _out_proj(xf, delta_out.reshape(t, d_a), lru_out.reshape(t, d_b), p["w_out"], layer,
                   _tile(t, 1024), _tile(d, 1024))
    f = p["w_mlp_up"].shape[2]
    hid = _mlp_up(x1, p["w_norm_mlp"], p["w_mlp_up"], layer, tm, _tile(f, 1024))
    x2 = _mlp_down(x1, hid, p["w_mlp_down"], layer, _tile(t, 1024), _tile(d, 1024), _tile(f, 4096))
    if final_norm:
        x2 = _final_norm(x2, w_final, tm)
    return x2.reshape(b, l, d), s_d_new, conv_d_new, h_new[:, 0, :], conv_l_new


def _trunk(x, s_d, conv_d, h_l, conv_l, layers, w_final):
    sds, cds, hls, cls = [], [], [], []
    depth = len(layers)
    for i, p in enumerate(layers):
        x, sd, cd, hl, cl = _layer(x, conv_d[i], s_d[i], conv_l[i], h_l[i], p, w_final, i == depth - 1)
        sds.append(sd)
        cds.append(cd)
        hls.append(hl)
        cls.append(cl)
    return x, jnp.stack(sds), jnp.stack(cds), jnp.stack(hls), jnp.stack(cls)


def kernel(x_prompt, x_sample, state_delta, state_conv_delta, state_lru, state_conv_lru,
           w_norm_mix, w_in, w_conv_delta, a_log, dt_bias, w_norm_delta,
           w_conv_lru, b_conv_lru, w_gate_a, b_gate_a, w_gate_x, b_gate_x,
           lam, w_norm_lru, w_out, w_norm_mlp, w_mlp_up, w_mlp_down, w_norm_final):
    params = dict(w_norm_mix=w_norm_mix, w_in=w_in, w_conv_delta=w_conv_delta, a_log=a_log, dt_bias=dt_bias,
                  w_norm_delta=w_norm_delta, w_conv_lru=w_conv_lru, b_conv_lru=b_conv_lru,
                  w_gate_a=w_gate_a, b_gate_a=b_gate_a, w_gate_x=w_gate_x, b_gate_x=b_gate_x,
                  lam=lam, w_norm_lru=w_norm_lru, w_out=w_out, w_norm_mlp=w_norm_mlp,
                  w_mlp_up=w_mlp_up, w_mlp_down=w_mlp_down)
    depth = w_in.shape[0]
    stacked = _prep_stacked(params)
    layers = [_prep_layer(i, params, stacked) for i in range(depth)]
    w_final = w_norm_final[None, :]

    bp = x_prompt.shape[0]
    z_sd = jnp.zeros((depth, bp) + state_delta.shape[2:], F32)
    z_cd = jnp.zeros((depth, bp) + state_conv_delta.shape[2:], x_prompt.dtype)
    z_hl = jnp.zeros((depth, bp) + state_lru.shape[2:], F32)
    z_cl = jnp.zeros((depth, bp) + state_conv_lru.shape[2:], x_prompt.dtype)
    outs_p = _trunk(x_prompt, z_sd, z_cd, z_hl, z_cl, layers, w_final)
    outs_s = _trunk(x_sample, state_delta, state_conv_delta, state_lru, state_conv_lru, layers, w_final)
    return (outs_p[0], outs_s[0]) + outs_p[1:] + outs_s[1:]
```

```python
import functools

import jax
import jax.numpy as jnp
from jax import lax
from jax.experimental import pallas as pl
from jax.experimental.pallas import tpu as pltpu

HEAD_DIM = 128
LRU_BLOCK = 128
CONV_W = 4
LRU_C = 8.0
EPS = 1e-6
LANES = 128
SUBLANES = 8
CHUNK = 128
NILPOTENT_BLOCK = 16
GROUP_CHUNKS = 4
DELTA_UNITS = 16
SQRT_FLOOR = 1e-30
VMEM_LIMIT = 56 * 1024 * 1024

F32 = jnp.float32
BF16 = jnp.bfloat16


def _cparams(sem):
    return pltpu.CompilerParams(dimension_semantics=sem, vmem_limit_bytes=VMEM_LIMIT)


def _dot(a, b):
    return jnp.dot(a, b, preferred_element_type=F32)


def _dot_nt(a, b):
    return lax.dot_general(a, b, (((1,), (1,)), ((), ())), preferred_element_type=F32)


def _split_bf16(x):
    hi = x.astype(BF16)
    lo = (x - hi.astype(F32)).astype(BF16)
    return hi, lo


def _dot3(xs, ys):
    (xh, xl), (yh, yl) = xs, ys
    lhs = jnp.concatenate([xh, xl, xh], axis=1)
    rhs = jnp.concatenate([yh, yh, yl], axis=0)
    return _dot(lhs, rhs)


def _cumsum_rows(tril16, x):
    hi, lo = _split_bf16(x)
    lo2 = (x - hi.astype(F32) - lo.astype(F32)).astype(BF16)
    lhs = jnp.concatenate([tril16, tril16, tril16], axis=1)
    rhs = jnp.concatenate([hi, lo, lo2], axis=0)
    return _dot(lhs, rhs)


def _sigmoid(x):
    return 0.5 * jnp.tanh(0.5 * x) + 0.5


def _softplus(x):
    return jnp.maximum(x, 0.0) + jnp.log1p(jnp.exp(-jnp.abs(x)))


def _silu(x):
    return x * _sigmoid(x)


def _gelu_tanh(x):
    c = 0.7978845608028654
    return 0.5 * x * (1.0 + jnp.tanh(c * (x + 0.044715 * (x * x * x))))


def _rms_rows(x, w):
    ms = jnp.mean(x * x, axis=-1, keepdims=True)
    return (x * lax.rsqrt(ms + EPS)) * w


def _norm_proj_kernel(x_ref, wn_ref, w_ref, wg_ref, arow_ref, dtrow_ref, o_ref, g_ref, xn_ref, *, heads):
    @pl.when(pl.program_id(1) == 0)
    def _():
        xn_ref[...] = _rms_rows(x_ref[...], wn_ref[...]).astype(BF16)
        logits = _dot(xn_ref[...], wg_ref[...])
        lane = lax.broadcasted_iota(jnp.int32, logits.shape, 1)
        log_decay = -jnp.exp(arow_ref[...]) * _softplus(logits + dtrow_ref[...])
        g_ref[...] = jnp.where(lane < heads, _sigmoid(logits), log_decay)

    o_ref[...] = _dot(xn_ref[...], w_ref[...])


def _norm_proj(x, wn, w_main, w_gate, a_row, dt_row, layer, heads, tm, tn):
    t, d = x.shape
    n = w_main.shape[2]
    row_spec = pl.BlockSpec((1, LANES), lambda i, j: (0, 0))
    return pl.pallas_call(
        functools.partial(_norm_proj_kernel, heads=heads),
        grid=(t // tm, n // tn),
        in_specs=[
            pl.BlockSpec((tm, d), lambda i, j: (i, 0)),
            pl.BlockSpec((1, d), lambda i, j: (0, 0)),
            pl.BlockSpec((None, d, tn), lambda i, j: (layer, 0, j)),
            pl.BlockSpec((None, d, LANES), lambda i, j: (layer, 0, 0)),
            row_spec, row_spec,
        ],
        out_specs=[
            pl.BlockSpec((tm, tn), lambda i, j: (i, j)),
            pl.BlockSpec((tm, LANES), lambda i, j: (i, 0)),
        ],
        out_shape=[jax.ShapeDtypeStruct((t, n), F32), jax.ShapeDtypeStruct((t, LANES), F32)],
        scratch_shapes=[pltpu.VMEM((tm, d), BF16)],
        compiler_params=_cparams(("arbitrary", "arbitrary")),
        name="norm_proj",
    )(x, wn, w_main, w_gate, a_row, dt_row)


def _conv_block(ext_ref, x, prev_ref, w_ref, first, rows):
    lo = SUBLANES - (CONV_W - 1)

    @pl.when(first)
    def _():
        ext_ref[lo:SUBLANES, :] = prev_ref[0]

    @pl.when(jnp.logical_not(first))
    def _():
        ext_ref[lo:SUBLANES, :] = ext_ref[rows + lo:rows + SUBLANES, :]

    ext_ref[SUBLANES:SUBLANES + rows, :] = x
    y = ext_ref[lo:lo + rows, :] * w_ref[0:1, :]
    for j in range(1, CONV_W):
        y = y + ext_ref[lo + j:lo + j + rows, :] * w_ref[j:j + 1, :]
    return y


def _unit_lower_inverse_stages(mats, row, col, live, out):
    n = min(mats[0].shape[0], live)
    eye = (row == col).astype(F32)
    shift = NILPOTENT_BLOCK.bit_length() - 1
    same = (row >> shift) == (col >> shift)
    ps = [jnp.where(same, a, 0.0) for a in mats]
    ts = [eye - p for p in ps]
    pss = [_split_bf16(p) for p in ps]
    size = 2
    while size < NILPOTENT_BLOCK:
        pss = [_split_bf16(_dot3(p, p)) for p in pss]
        yield
        ts = [t + _dot3(_split_bf16(t), p) for t, p in zip(ts, pss)]
        yield
        size *= 2
    blk = NILPOTENT_BLOCK
    tss = [_split_bf16(t) for t in ts]
    while blk < n:
        shift += 1
        nxt = (row >> shift) == (col >> shift)
        sel = jnp.logical_and(nxt, jnp.logical_not(same))
        nss = [_split_bf16(_dot3(t, _split_bf16(jnp.where(sel, a, 0.0)))) for t, a in zip(tss, mats)]
        yield
        ts = [t - _dot3(nm, t2) for t, nm, t2 in zip(ts, nss, tss)]
        tss = [_split_bf16(t) for t in ts]
        yield
        same = nxt
        blk *= 2
    out.extend(tss)


def _interleave(main, side, every):
    for count, _ in enumerate(main, 1):
        if count % every == 0:
            next(side, None)
    for _ in side:
        pass


def _delta_kernel(q_ref, k_ref, v_ref, z_ref, g_ref, pq_ref, pk_ref, pv_ref, s0_ref,
                  wq_ref, wk_ref, wv_ref, wnorm_ref,
                  o_ref, s_ref, eq_ref, ek_ref, ev_ref, st_ref, *, rows, nblk, valid, heads, hpb):
    head0 = pl.program_id(1) * hpb
    l = pl.program_id(2)
    first = l == 0

    @pl.when(first)
    def _():
        st_ref[...] = s0_ref[0]

    q_all = _silu(_conv_block(eq_ref, q_ref[0], pq_ref, wq_ref, first, rows))
    k_all = _silu(_conv_block(ek_ref, k_ref[0], pk_ref, wk_ref, first, rows))
    v_all = _silu(_conv_block(ev_ref, v_ref[0], pv_ref, wv_ref, first, rows))
    zg = _silu(z_ref[0])

    gates = g_ref[...]
    lane = lax.broadcasted_iota(jnp.int32, gates.shape, 1)
    if valid < rows * nblk:
        live_rows = l * rows + lax.broadcasted_iota(jnp.int32, (rows, 1), 0) < valid
    per_head = []
    for j in range(hpb):
        hs = slice(j * HEAD_DIM, (j + 1) * HEAD_DIM)
        q, k = q_all[:, hs], k_all[:, hs]
        q = q * lax.rsqrt(jnp.sum(q * q, axis=-1, keepdims=True) + EPS) * (HEAD_DIM ** -0.5)
        k = k * lax.rsqrt(jnp.sum(k * k, axis=-1, keepdims=True) + EPS)
        beta = jnp.sum(jnp.where(lane == head0 + j, gates, 0.0), axis=1, keepdims=True)
        g = jnp.sum(jnp.where(lane == heads + head0 + j, gates, 0.0), axis=1, keepdims=True)
        if valid < rows * nblk:
            beta = jnp.where(live_rows, beta, 0.0)
            g = jnp.where(live_rows, g, 0.0)
            k = jnp.where(live_rows, k, 0.0)
        per_head.append((q, k, v_all[:, hs], beta, g))

    row = lax.broadcasted_iota(jnp.int32, (CHUNK, CHUNK), 0)
    col = lax.broadcasted_iota(jnp.int32, (CHUNK, CHUNK), 1)
    causal = row >= col
    strict = row > col
    tril16 = causal.astype(BF16)

    live = min(CHUNK, valid) if nblk * rows == CHUNK else CHUNK

    def prepare(units, out):
        pre = []
        for j, c in units:
            q, k, v, beta, g = per_head[j]
            sl = slice(c * CHUNK, (c + 1) * CHUNK)
            qc, kc, vc = q[sl], k[sl], v[sl]
            beta_b = jnp.broadcast_to(beta[sl], (CHUNK, CHUNK))
            g_b = jnp.broadcast_to(g[sl], (CHUNK, CHUNK))
            gcum = _cumsum_rows(tril16, g_b)
            decay = jnp.where(causal, jnp.exp(jnp.where(causal, gcum - gcum.T, 0.0)), 0.0)
            eg = jnp.exp(gcum)
            kb = kc * beta_b
            kc16 = kc.astype(BF16)
            a_mat = jnp.where(strict, _dot_nt(kb.astype(BF16), kc16) * decay, 0.0)
            g_last = gcum[CHUNK - 1:CHUNK, :]
            pre.append(dict(
                a=a_mat, rhs=jnp.concatenate([vc * beta_b, kb * eg], axis=1),
                qk=(_dot_nt(qc.astype(BF16), kc16) * decay).astype(BF16), qg=qc * eg,
                kdt=(kc * jnp.exp(g_last - gcum)).T.astype(BF16), gl=jnp.exp(g_last)))
            yield
        t_invs = []
        yield from _unit_lower_inverse_stages([p["a"] for p in pre], row, col, live, t_invs)
        sols = [_dot3(t, _split_bf16(p["rhs"])).astype(BF16) for t, p in zip(t_invs, pre)]
        yield
        for unit, p, sol in zip(units, pre, sols):
            ks = _dot(p["kdt"], sol)
            qs = _dot(p["qk"], sol)
            lhs = jnp.concatenate([p["qg"] - qs[:, HEAD_DIM:], ks[:, HEAD_DIM:]], axis=0).astype(BF16)
            out[unit] = (lhs, qs[:, :HEAD_DIM], ks[:, :HEAD_DIM], p["gl"])
        yield

    state = [st_ref[j] for j in range(hpb)]

    def recur(units, ops):
        for j, c in units:
            sl = slice(c * CHUNK, (c + 1) * CHUNK)
            hs = slice(j * HEAD_DIM, (j + 1) * HEAD_DIM)
            lhs, o0, s_add, gl = ops[(j, c)]
            s = state[j]
            both = _dot(lhs, s.astype(BF16))
            o = both[:CHUNK] + o0
            state[j] = (s * gl - both[CHUNK:]) + s_add
            o_ref[0, sl, hs] = (_rms_rows(o, wnorm_ref[...]) * zg[sl, hs]).astype(o_ref.dtype)
            yield

    units = [(j, c) for j in range(hpb) for c in range(rows // CHUNK)]
    n = len(units)
    sizes = [n] if n <= GROUP_CHUNKS else [n // 2] + [GROUP_CHUNKS] * (n // 2 // GROUP_CHUNKS)
    n_products = 2 * (NILPOTENT_BLOCK.bit_length() - 2) + 2 * ((CHUNK // NILPOTENT_BLOCK).bit_length() - 1)
    ops = {}
    pending, n_pending, start = iter(()), 1, 0
    for size in sizes:
        group = units[start:start + size]
        start += size
        every = max(1, (size + n_products + 2) // n_pending)
        _interleave(prepare(group, ops), pending, every)
        pending, n_pending = recur(group, ops), size
    for _ in pending:
        pass
    for j in range(hpb):
        st_ref[j] = state[j]

    @pl.when(l == nblk - 1)
    def _():
        s_ref[0] = st_ref[...]


def _delta_mixer(proj, gates, conv_prev, s0, w_conv, w_norm, *, valid, rows, hpb):
    b, l, _ = proj.shape
    heads = s0.shape[1]
    nl = l // rows
    ng = heads // hpb
    width = hpb * HEAD_DIM
    blk = lambda off: pl.BlockSpec((1, rows, width), lambda bi, gi, li, off=off: (bi, li, off * ng + gi))
    prev = lambda off: pl.BlockSpec((1, CONV_W - 1, width), lambda bi, gi, li, off=off: (bi, 0, off * ng + gi))
    wcv = lambda off: pl.BlockSpec((CONV_W, width), lambda bi, gi, li, off=off: (0, off * ng + gi))
    row_spec = pl.BlockSpec((1, LANES), lambda bi, gi, li: (0, 0))
    state_spec = pl.BlockSpec((1, hpb, HEAD_DIM, HEAD_DIM), lambda bi, gi, li: (bi, gi, 0, 0))
    kern = functools.partial(_delta_kernel, rows=rows, nblk=nl, valid=valid, heads=heads, hpb=hpb)
    return pl.pallas_call(
        kern,
        grid=(b, ng, nl),
        in_specs=[blk(0), blk(1), blk(2), blk(3),
                  pl.BlockSpec((rows, LANES), lambda bi, gi, li: (bi * nl + li, 0)),
                  prev(0), prev(1), prev(2), state_spec,
                  wcv(0), wcv(1), wcv(2), row_spec],
        out_specs=[pl.BlockSpec((1, rows, width), lambda bi, gi, li: (bi, li, gi)), state_spec],
        out_shape=[jax.ShapeDtypeStruct((b, l, heads * HEAD_DIM), BF16),
                   jax.ShapeDtypeStruct(s0.shape, F32)],
        scratch_shapes=[pltpu.VMEM((rows + SUBLANES, width), F32)] * 3
        + [pltpu.VMEM((hpb, HEAD_DIM, HEAD_DIM), F32)],
        compiler_params=_cparams(("arbitrary", "arbitrary", "arbitrary")),
        name="delta_mixer",
    )(proj, proj, proj, proj, gates, conv_prev, conv_prev, conv_prev, s0,
      w_conv, w_conv, w_conv, w_norm)


def _lru_kernel(x_ref, y_ref, prev_ref, h0_ref, wc_ref, bc_ref, wa_ref, ba_ref, wx_ref, bx_ref,
                lam_ref, wn_ref, o_ref, hl_ref, ext_ref, a_ref, b_ref, h_ref, *, rows, valid, groups):
    l = pl.program_id(1)
    first = l == 0

    @pl.when(first)
    def _():
        h_ref[...] = h0_ref[0]

    xc = _conv_block(ext_ref, x_ref[0], prev_ref, wc_ref, first, rows) + bc_ref[...]
    sub = lax.broadcasted_iota(jnp.int32, (rows // SUBLANES, SUBLANES, LRU_BLOCK), 1)
    for gi in range(groups):
        cs = slice(gi * LRU_BLOCK, (gi + 1) * LRU_BLOCK)
        xg = xc[:, cs]
        xg16 = xg.astype(BF16)
        gate_r = _sigmoid(_dot(xg16, wa_ref[gi]) + ba_ref[:, cs])
        gate_i = _sigmoid(_dot(xg16, wx_ref[gi]) + bx_ref[:, cs])
        log_a = -LRU_C * gate_r * _softplus(-lam_ref[:, cs])
        a = jnp.exp(log_a)
        var = 1.0 - jnp.exp(2.0 * log_a)
        bb = (var * lax.rsqrt(jnp.maximum(var, SQRT_FLOOR))) * gate_i * xg
        a = a.reshape(rows // SUBLANES, SUBLANES, LRU_BLOCK)
        bb = bb.reshape(rows // SUBLANES, SUBLANES, LRU_BLOCK)
        for s in (1, 2, 4):
            keep = sub >= s
            a_sh = jnp.where(keep, pltpu.roll(a, s, 1), 1.0)
            b_sh = jnp.where(keep, pltpu.roll(bb, s, 1), 0.0)
            bb = a * b_sh + bb
            a = a * a_sh
        a_ref[:, cs] = a.reshape(rows, LRU_BLOCK)
        b_ref[:, cs] = bb.reshape(rows, LRU_BLOCK)

    def step(i, h_prev):
        r0 = pl.multiple_of(i * SUBLANES, SUBLANES)
        hb = a_ref[pl.ds(r0, SUBLANES), :] * h_prev + b_ref[pl.ds(r0, SUBLANES), :]
        b_ref[pl.ds(r0, SUBLANES), :] = hb
        return hb[SUBLANES - 1:SUBLANES, :]

    h_ref[...] = lax.fori_loop(0, rows // SUBLANES, step, h_ref[...])

    hs = b_ref[...]
    last_blk, last_row = divmod(valid - 1, rows)

    @pl.when(l == last_blk)
    def _():
        hl_ref[0] = hs[last_row:last_row + 1, :]

    y = _gelu_tanh(y_ref[0]) * hs
    o_ref[0] = _rms_rows(y, wn_ref[...]).astype(o_ref.dtype)


def _lru_mixer(proj, conv_prev, h0, w_conv, b_conv, w_ga, b_ga, w_gx, b_gx, lam, w_norm,
               *, valid, rows, x_blk, y_blk):
    b, l, _ = proj.shape
    d = h0.shape[-1]
    groups = d // LRU_BLOCK
    nl = l // rows
    vec = pl.BlockSpec((1, d), lambda bi, li: (0, 0))
    gw = pl.BlockSpec((groups, LRU_BLOCK, LRU_BLOCK), lambda bi, li: (0, 0, 0))
    kern = functools.partial(_lru_kernel, rows=rows, valid=valid, groups=groups)
    return pl.pallas_call(
        kern,
        grid=(b, nl),
        in_specs=[pl.BlockSpec((1, rows, d), lambda bi, li: (bi, li, x_blk)),
                  pl.BlockSpec((1, rows, d), lambda bi, li: (bi, li, y_blk)),
                  pl.BlockSpec((1, CONV_W - 1, d), lambda bi, li: (bi, 0, 0)),
                  pl.BlockSpec((1, 1, d), lambda bi, li: (bi, 0, 0)),
                  pl.BlockSpec((CONV_W, d), lambda bi, li: (0, 0)),
                  vec, gw, vec, gw, vec, vec, vec],
        out_specs=[pl.BlockSpec((1, rows, d), lambda bi, li: (bi, li, 0)),
                   pl.BlockSpec((1, 1, d), lambda bi, li: (bi, 0, 0))],
        out_shape=[jax.ShapeDtypeStruct((b, l, d), BF16), jax.ShapeDtypeStruct((b, 1, d), F32)],
        scratch_shapes=[pltpu.VMEM((rows + SUBLANES, d), F32), pltpu.VMEM((rows, d), F32),
                        pltpu.VMEM((rows, d), F32), pltpu.VMEM((1, d), F32)],
        compiler_params=_cparams(("arbitrary", "arbitrary")),
        name="lru_mixer",
    )(proj, proj, conv_prev, h0, w_conv, b_conv, w_ga, b_ga, w_gx, b_gx, lam, w_norm)


def _out_proj_kernel(x_ref, a_ref, b_ref, wa_ref, wb_ref, o_ref):
    o_ref[...] = x_ref[...] + _dot(a_ref[...], wa_ref[...]) + _dot(b_ref[...], wb_ref[...])


def _out_proj(x, mix_a, mix_b, w_out, layer, tm, tn):
    t, d = x.shape
    da, db = mix_a.shape[1], mix_b.shape[1]
    assert da == db
    return pl.pallas_call(
        _out_proj_kernel,
        grid=(t // tm, d // tn),
        in_specs=[pl.BlockSpec((tm, tn), lambda i, j: (i, j)),
                  pl.BlockSpec((tm, da), lambda i, j: (i, 0)),
                  pl.BlockSpec((tm, db), lambda i, j: (i, 0)),
                  pl.BlockSpec((None, da, tn), lambda i, j: (layer, 0, j)),
                  pl.BlockSpec((None, db, tn), lambda i, j: (layer, 1, j))],
        out_specs=pl.BlockSpec((tm, tn), lambda i, j: (i, j)),
        out_shape=jax.ShapeDtypeStruct((t, d), F32),
        compiler_params=_cparams(("arbitrary", "arbitrary")),
        name="out_proj",
    )(x, mix_a, mix_b, w_out, w_out)


def _mlp_up_kernel(x_ref, wn_ref, wu_ref, h_ref, xn_ref):
    @pl.when(pl.program_id(1) == 0)
    def _():
        xn_ref[...] = _rms_rows(x_ref[...], wn_ref[...]).astype(BF16)

    hid = jnp.maximum(_dot(xn_ref[...], wu_ref[...]), 0.0)
    h_ref[...] = (hid * hid).astype(BF16)


def _mlp_up(x, wn, w_up, layer, tm, tn):
    t, d = x.shape
    f = w_up.shape[2]
    return pl.pallas_call(
        _mlp_up_kernel,
        grid=(t // tm, f // tn),
        in_specs=[pl.BlockSpec((tm, d), lambda i, j: (i, 0)),
                  pl.BlockSpec((1, d), lambda i, j: (0, 0)),
                  pl.BlockSpec((None, d, tn), lambda i, j: (layer, 0, j))],
        out_specs=pl.BlockSpec((tm, tn), lambda i, j: (i, j)),
        out_shape=jax.ShapeDtypeStruct((t, f), BF16),
        scratch_shapes=[pltpu.VMEM((tm, d), BF16)],
        compiler_params=_cparams(("arbitrary", "arbitrary")),
        name="mlp_up",
    )(x, wn, w_up)


def _mlp_down_kernel(x_ref, h_ref, wd_ref, o_ref):
    @pl.when(pl.program_id(2) == 0)
    def _():
        o_ref[...] = x_ref[...]

    o_ref[...] += _dot(h_ref[...], wd_ref[...])


def _mlp_down(x, hid, w_down, layer, tm, tn, tk):
    t, d = x.shape
    f = hid.shape[1]
    return pl.pallas_call(
        _mlp_down_kernel,
        grid=(t // tm, d // tn, f // tk),
        in_specs=[pl.BlockSpec((tm, tn), lambda i, j, k: (i, j)),
                  pl.BlockSpec((tm, tk), lambda i, j, k: (i, k)),
                  pl.BlockSpec((None, tk, tn), lambda i, j, k: (layer, k, j))],
        out_specs=pl.BlockSpec((tm, tn), lambda i, j, k: (i, j)),
        out_shape=jax.ShapeDtypeStruct((t, d), F32),
        compiler_params=_cparams(("arbitrary", "arbitrary", "arbitrary")),
        name="mlp_down",
    )(x, hid, w_down)


def _final_norm_kernel(x_ref, w_ref, o_ref):
    o_ref[...] = _rms_rows(x_ref[...], w_ref[...])


def _final_norm(x, w, tm):
    t, d = x.shape
    return pl.pallas_call(
        _final_norm_kernel,
        grid=(t // tm,),
        in_specs=[pl.BlockSpec((tm, d), lambda i: (i, 0)), pl.BlockSpec((1, d), lambda i: (0, 0))],
        out_specs=pl.BlockSpec((tm, d), lambda i: (i, 0)),
        out_shape=jax.ShapeDtypeStruct((t, d), F32),
        compiler_params=_cparams(("arbitrary",)),
        name="final_norm",
    )(x, w)


def _tile(n, pref):
    t = min(n, pref)
    while n % t:
        t //= 2
    return t


def _prep_stacked(p):
    d_a = p["w_conv_delta"].shape[-1] // 3
    heads = p["a_log"].shape[-1]
    w_in = lax.optimization_barrier(p["w_in"].astype(BF16))
    gate_lo, gate_hi = 4 * d_a, 4 * d_a + 2 * heads
    return dict(
        w_main=jnp.concatenate([w_in[:, :, :gate_lo], w_in[:, :, gate_hi:]], axis=2),
        w_gate=jnp.pad(w_in[:, :, gate_lo:gate_hi], ((0, 0), (0, 0), (0, LANES - 2 * heads))),
        w_out=p["w_out"].astype(BF16), w_mlp_up=p["w_mlp_up"].astype(BF16),
        w_mlp_down=p["w_mlp_down"].astype(BF16))


def _prep_layer(i, p, stacked):
    heads = p["a_log"].shape[-1]
    row = lambda v: jnp.pad(v, (heads, LANES - 2 * heads))[None, :].astype(F32)
    return dict(
        stacked, layer=i,
        w_norm_mix=p["w_norm_mix"][i][None, :],
        w_conv_delta=p["w_conv_delta"][i], a_row=row(p["a_log"][i]), dt_row=row(p["dt_bias"][i]),
        w_norm_delta=p["w_norm_delta"][i][None, :],
        w_conv_lru=p["w_conv_lru"][i], b_conv_lru=p["b_conv_lru"][i][None, :],
        w_gate_a=p["w_gate_a"][i].astype(BF16), b_gate_a=p["b_gate_a"][i][None, :],
        w_gate_x=p["w_gate_x"][i].astype(BF16), b_gate_x=p["b_gate_x"][i][None, :],
        lam=p["lam"][i][None, :], w_norm_lru=p["w_norm_lru"][i][None, :],
        w_norm_mlp=p["w_norm_mlp"][i][None, :],
    )


def _layer(x, conv_d, s_d, conv_l, h_l, p, w_final, final_norm):
    b, l, d = x.shape
    t = b * l
    d_a = p["w_conv_delta"].shape[-1] // 3
    d_b = p["w_conv_lru"].shape[-1]
    assert d_a == d_b and d_a % HEAD_DIM == 0
    layer = p["layer"]
    tm = _tile(t, 512)
    xf = x.reshape(t, d)

    proj, gates = _norm_proj(xf, p["w_norm_mix"], p["w_main"], p["w_gate"], p["a_row"], p["dt_row"], layer,
                             d_a // HEAD_DIM, tm, _tile(p["w_main"].shape[2], 1024))
    n_main = proj.shape[1]
    proj = proj.reshape(b, l, n_main)
    conv_d_new = jnp.concatenate([conv_d, proj[:, :, :3 * d_a]], axis=1)[:, -(CONV_W - 1):]
    conv_l_new = jnp.concatenate([conv_l, proj[:, :, 4 * d_a:4 * d_a + d_b]], axis=1)[:, -(CONV_W - 1):]

    lp = -(-l // CHUNK) * CHUNK
    if lp != l:
        proj = jnp.pad(proj, ((0, 0), (0, lp - l), (0, 0)))
        gates = jnp.pad(gates.reshape(b, l, LANES), ((0, 0), (0, lp - l), (0, 0))).reshape(b * lp, LANES)

    delta_rows = _tile(lp, DELTA_UNITS * CHUNK)
    delta_out, s_d_new = _delta_mixer(
        proj, gates, conv_d, s_d, p["w_conv_delta"], p["w_norm_delta"],
        valid=l, rows=delta_rows, hpb=_tile(d_a // HEAD_DIM, DELTA_UNITS * CHUNK // delta_rows))
    lru_out, h_new = _lru_mixer(
        proj, conv_l, h_l[:, None, :], p["w_conv_lru"], p["b_conv_lru"], p["w_gate_a"], p["b_gate_a"],
        p["w_gate_x"], p["b_gate_x"], p["lam"], p["w_norm_lru"],
        valid=l, rows=_tile(lp, 256), x_blk=4 * d_a // d_b, y_blk=4 * d_a // d_b + 1)
    if lp != l:
        delta_out, lru_out = delta_out[:, :l], lru_out[:, :l]

    x1 = _out_proj(xf, delta_out.reshape(t, d_a), lru_out.reshape(t, d_b), p["w_out"], layer,
                   _tile(t, 1024), _tile(d, 1024))
    f = p["w_mlp_up"].shape[2]
    hid = _mlp_up(x1, p["w_norm_mlp"], p["w_mlp_up"], layer, tm, _tile(f, 1024))
    x2 = _mlp_down(x1, hid, p["w_mlp_down"], layer, _tile(t, 1024), _tile(d, 1024), _tile(f, 4096))
    if final_norm:
        x2 = _final_norm(x2, w_final, tm)
    return x2.reshape(b, l, d), s_d_new, conv_d_new, h_new[:, 0, :], conv_l_new


def _trunk(x, s_d, conv_d, h_l, conv_l, layers, w_final):
    sds, cds, hls, cls = [], [], [], []
    depth = len(layers)
    for i, p in enumerate(layers):
        x, sd, cd, hl, cl = _layer(x, conv_d[i], s_d[i], conv_l[i], h_l[i], p, w_final, i == depth - 1)
        sds.append(sd)
        cds.append(cd)
        hls.append(hl)
        cls.append(cl)
    return x, jnp.stack(sds), jnp.stack(cds), jnp.stack(hls), jnp.stack(cls)


def kernel(x_prompt, x_sample, state_delta, state_conv_delta, state_lru, state_conv_lru,
           w_norm_mix, w_in, w_conv_delta, a_log, dt_bias, w_norm_delta,
           w_conv_lru, b_conv_lru, w_gate_a, b_gate_a, w_gate_x, b_gate_x,
           lam, w_norm_lru, w_out, w_norm_mlp, w_mlp_up, w_mlp_down, w_norm_final):
    params = dict(w_norm_mix=w_norm_mix, w_in=w_in, w_conv_delta=w_conv_delta, a_log=a_log, dt_bias=dt_bias,
                  w_norm_delta=w_norm_delta, w_conv_lru=w_conv_lru, b_conv_lru=b_conv_lru,
                  w_gate_a=w_gate_a, b_gate_a=b_gate_a, w_gate_x=w_gate_x, b_gate_x=b_gate_x,
                  lam=lam, w_norm_lru=w_norm_lru, w_out=w_out, w_norm_mlp=w_norm_mlp,
                  w_mlp_up=w_mlp_up, w_mlp_down=w_mlp_down)
    depth = w_in.shape[0]
    stacked = _prep_stacked(params)
    layers = [_prep_layer(i, params, stacked) for i in range(depth)]
    w_final = w_norm_final[None, :]

    bp = x_prompt.shape[0]
    z_sd = jnp.zeros((depth, bp) + state_delta.shape[2:], F32)
    z_cd = jnp.zeros((depth, bp) + state_conv_delta.shape[2:], x_prompt.dtype)
    z_hl = jnp.zeros((depth, bp) + state_lru.shape[2:], F32)
    z_cl = jnp.zeros((depth, bp) + state_conv_lru.shape[2:], x_prompt.dtype)
    outs_p = _trunk(x_prompt, z_sd, z_cd, z_hl, z_cl, layers, w_final)
    outs_s = _trunk(x_sample, state_delta, state_conv_delta, state_lru, state_conv_lru, layers, w_final)
    return (outs_p[0], outs_s[0]) + outs_p[1:] + outs_s[1:]
```

```python
import functools

import jax
import jax.numpy as jnp
from jax import lax
from jax.experimental import pallas as pl
from jax.experimental.pallas import tpu as pltpu

HEAD_DIM = 128
LRU_BLOCK = 128
CONV_W = 4
LRU_C = 8.0
EPS = 1e-6
LANES = 128
SUBLANES = 8
CHUNK = 128
NILPOTENT_BLOCK = 16
GROUP_CHUNKS = 4
DELTA_UNITS = 16
SQRT_FLOOR = 1e-30
VMEM_LIMIT = 60 * 1024 * 1024

F32 = jnp.float32
BF16 = jnp.bfloat16


def _cparams(sem):
    return pltpu.CompilerParams(dimension_semantics=sem, vmem_limit_bytes=VMEM_LIMIT)


def _dot(a, b):
    return jnp.dot(a, b, preferred_element_type=F32)


def _dot_nt(a, b):
    return lax.dot_general(a, b, (((1,), (1,)), ((), ())), preferred_element_type=F32)


def _split_bf16(x):
    hi = x.astype(BF16)
    lo = (x - hi.astype(F32)).astype(BF16)
    return hi, lo


def _dot3(xs, ys):
    (xh, xl), (yh, yl) = xs, ys
    lhs = jnp.concatenate([xh, xl, xh], axis=1)
    rhs = jnp.concatenate([yh, yh, yl], axis=0)
    return _dot(lhs, rhs)


def _cumsum_rows(tril16, x):
    hi, lo = _split_bf16(x)
    lo2 = (x - hi.astype(F32) - lo.astype(F32)).astype(BF16)
    lhs = jnp.concatenate([tril16, tril16, tril16], axis=1)
    rhs = jnp.concatenate([hi, lo, lo2], axis=0)
    return _dot(lhs, rhs)


def _sigmoid(x):
    return 0.5 * jnp.tanh(0.5 * x) + 0.5


def _softplus(x):
    return jnp.maximum(x, 0.0) + jnp.log1p(jnp.exp(-jnp.abs(x)))


def _silu(x):
    return x * _sigmoid(x)


def _gelu_tanh(x):
    c = 0.7978845608028654
    return 0.5 * x * (1.0 + jnp.tanh(c * (x + 0.044715 * (x * x * x))))


def _rms_rows(x, w):
    ms = jnp.mean(x * x, axis=-1, keepdims=True)
    return (x * lax.rsqrt(ms + EPS)) * w


def _norm_proj_kernel(x_ref, wn_ref, w_ref, wg_ref, arow_ref, dtrow_ref, o_ref, g_ref, xn_ref, *, heads):
    @pl.when(pl.program_id(1) == 0)
    def _():
        xn_ref[...] = _rms_rows(x_ref[...], wn_ref[...]).astype(BF16)
        logits = _dot(xn_ref[...], wg_ref[...])
        lane = lax.broadcasted_iota(jnp.int32, logits.shape, 1)
        log_decay = -jnp.exp(arow_ref[...]) * _softplus(logits + dtrow_ref[...])
        g_ref[...] = jnp.where(lane < heads, _sigmoid(logits), log_decay)

    o_ref[...] = _dot(xn_ref[...], w_ref[...])


def _norm_proj(x, wn, w_main, w_gate, a_row, dt_row, layer, heads, tm, tn):
    t, d = x.shape
    n = w_main.shape[2]
    row_spec = pl.BlockSpec((1, LANES), lambda i, j: (0, 0))
    return pl.pallas_call(
        functools.partial(_norm_proj_kernel, heads=heads),
        grid=(t // tm, n // tn),
        in_specs=[
            pl.BlockSpec((tm, d), lambda i, j: (i, 0)),
            pl.BlockSpec((1, d), lambda i, j: (0, 0)),
            pl.BlockSpec((None, d, tn), lambda i, j: (layer, 0, j)),
            pl.BlockSpec((None, d, LANES), lambda i, j: (layer, 0, 0)),
            row_spec, row_spec,
        ],
        out_specs=[
            pl.BlockSpec((tm, tn), lambda i, j: (i, j)),
            pl.BlockSpec((tm, LANES), lambda i, j: (i, 0)),
        ],
        out_shape=[jax.ShapeDtypeStruct((t, n), F32), jax.ShapeDtypeStruct((t, LANES), F32)],
        scratch_shapes=[pltpu.VMEM((tm, d), BF16)],
        compiler_params=_cparams(("arbitrary", "arbitrary")),
        name="norm_proj",
    )(x, wn, w_main, w_gate, a_row, dt_row)


def _conv_block(ext_ref, x, prev_ref, w_ref, first, rows):
    lo = SUBLANES - (CONV_W - 1)

    @pl.when(first)
    def _():
        ext_ref[lo:SUBLANES, :] = prev_ref[0]

    @pl.when(jnp.logical_not(first))
    def _():
        ext_ref[lo:SUBLANES, :] = ext_ref[rows + lo:rows + SUBLANES, :]

    ext_ref[SUBLANES:SUBLANES + rows, :] = x
    y = ext_ref[lo:lo + rows, :] * w_ref[0:1, :]
    for j in range(1, CONV_W):
        y = y + ext_ref[lo + j:lo + j + rows, :] * w_ref[j:j + 1, :]
    return y


def _unit_lower_inverse_stages(mats, row, col, live, out):
    n = min(mats[0].shape[0], live)
    eye = (row == col).astype(F32)
    shift = NILPOTENT_BLOCK.bit_length() - 1
    same = (row >> shift) == (col >> shift)
    ps = [jnp.where(same, a, 0.0) for a in mats]
    ts = [eye - p for p in ps]
    pss = [_split_bf16(p) for p in ps]
    size = 2
    while size < NILPOTENT_BLOCK:
        pss = [_split_bf16(_dot3(p, p)) for p in pss]
        yield
        ts = [t + _dot3(_split_bf16(t), p) for t, p in zip(ts, pss)]
        yield
        size *= 2
    blk = NILPOTENT_BLOCK
    tss = [_split_bf16(t) for t in ts]
    while blk < n:
        shift += 1
        nxt = (row >> shift) == (col >> shift)
        sel = jnp.logical_and(nxt, jnp.logical_not(same))
        nss = [_split_bf16(_dot3(t, _split_bf16(jnp.where(sel, a, 0.0)))) for t, a in zip(tss, mats)]
        yield
        ts = [t - _dot3(nm, t2) for t, nm, t2 in zip(ts, nss, tss)]
        tss = [_split_bf16(t) for t in ts]
        yield
        same = nxt
        blk *= 2
    out.extend(tss)


def _interleave(main, side, every):
    for count, _ in enumerate(main, 1):
        if count % every == 0:
            next(side, None)
    for _ in side:
        pass


def _delta_kernel(q_ref, k_ref, v_ref, z_ref, g_ref, pq_ref, pk_ref, pv_ref, s0_ref,
                  wq_ref, wk_ref, wv_ref, wnorm_ref,
                  o_ref, s_ref, eq_ref, ek_ref, ev_ref, st_ref, *, rows, nblk, valid, heads, hpb):
    head0 = pl.program_id(1) * hpb
    l = pl.program_id(2)
    first = l == 0

    @pl.when(first)
    def _():
        st_ref[...] = s0_ref[0]

    q_all = _silu(_conv_block(eq_ref, q_ref[0], pq_ref, wq_ref, first, rows))
    k_all = _silu(_conv_block(ek_ref, k_ref[0], pk_ref, wk_ref, first, rows))
    v_all = _silu(_conv_block(ev_ref, v_ref[0], pv_ref, wv_ref, first, rows))
    zg = _silu(z_ref[0])

    gates = g_ref[...]
    lane = lax.broadcasted_iota(jnp.int32, gates.shape, 1)
    if valid < rows * nblk:
        live_rows = l * rows + lax.broadcasted_iota(jnp.int32, (rows, 1), 0) < valid
    per_head = []
    for j in range(hpb):
        hs = slice(j * HEAD_DIM, (j + 1) * HEAD_DIM)
        q, k = q_all[:, hs], k_all[:, hs]
        q = q * lax.rsqrt(jnp.sum(q * q, axis=-1, keepdims=True) + EPS) * (HEAD_DIM ** -0.5)
        k = k * lax.rsqrt(jnp.sum(k * k, axis=-1, keepdims=True) + EPS)
        beta = jnp.sum(jnp.where(lane == head0 + j, gates, 0.0), axis=1, keepdims=True)
        g = jnp.sum(jnp.where(lane == heads + head0 + j, gates, 0.0), axis=1, keepdims=True)
        if valid < rows * nblk:
            beta = jnp.where(live_rows, beta, 0.0)
            g = jnp.where(live_rows, g, 0.0)
            k = jnp.where(live_rows, k, 0.0)
        per_head.append((q, k, v_all[:, hs], beta, g))

    row = lax.broadcasted_iota(jnp.int32, (CHUNK, CHUNK), 0)
    col = lax.broadcasted_iota(jnp.int32, (CHUNK, CHUNK), 1)
    causal = row >= col
    strict = row > col
    tril16 = causal.astype(BF16)

    live = min(CHUNK, valid) if nblk * rows == CHUNK else CHUNK

    def prepare(units, out):
        pre = []
        for j, c in units:
            q, k, v, beta, g = per_head[j]
            sl = slice(c * CHUNK, (c + 1) * CHUNK)
            qc, kc, vc = q[sl], k[sl], v[sl]
            beta_b = jnp.broadcast_to(beta[sl], (CHUNK, CHUNK))
            g_b = jnp.broadcast_to(g[sl], (CHUNK, CHUNK))
            gcum = _cumsum_rows(tril16, g_b)
            decay = jnp.where(causal, jnp.exp(jnp.where(causal, gcum - gcum.T, 0.0)), 0.0)
            eg = jnp.exp(gcum)
            kb = kc * beta_b
            kc16 = kc.astype(BF16)
            a_mat = jnp.where(strict, _dot_nt(kb.astype(BF16), kc16) * decay, 0.0)
            g_last = gcum[CHUNK - 1:CHUNK, :]
            pre.append(dict(
                a=a_mat, rhs=jnp.concatenate([vc * beta_b, kb * eg], axis=1),
                qk=(_dot_nt(qc.astype(BF16), kc16) * decay).astype(BF16), qg=qc * eg,
                kdt=(kc * jnp.exp(g_last - gcum)).T.astype(BF16), gl=jnp.exp(g_last)))
            yield
        t_invs = []
        yield from _unit_lower_inverse_stages([p["a"] for p in pre], row, col, live, t_invs)
        sols = [_dot3(t, _split_bf16(p["rhs"])).astype(BF16) for t, p in zip(t_invs, pre)]
        yield
        for unit, p, sol in zip(units, pre, sols):
            ks = _dot(p["kdt"], sol)
            qs = _dot(p["qk"], sol)
            lhs = jnp.concatenate([p["qg"] - qs[:, HEAD_DIM:], ks[:, HEAD_DIM:]], axis=0).astype(BF16)
            out[unit] = (lhs, qs[:, :HEAD_DIM], ks[:, :HEAD_DIM], p["gl"])
        yield

    state = [st_ref[j] for j in range(hpb)]

    def recur(units, ops):
        for j, c in units:
            sl = slice(c * CHUNK, (c + 1) * CHUNK)
            hs = slice(j * HEAD_DIM, (j + 1) * HEAD_DIM)
            lhs, o0, s_add, gl = ops[(j, c)]
            s = state[j]
            both = _dot(lhs, s.astype(BF16))
            o = both[:CHUNK] + o0
            state[j] = (s * gl - both[CHUNK:]) + s_add
            o_ref[0, sl, hs] = (_rms_rows(o, wnorm_ref[...]) * zg[sl, hs]).astype(o_ref.dtype)
            yield

    units = [(j, c) for j in range(hpb) for c in range(rows // CHUNK)]
    n = len(units)
    sizes = [n] if n <= GROUP_CHUNKS else [n // 2] + [GROUP_CHUNKS] * (n // 2 // GROUP_CHUNKS)
    n_products = 2 * (NILPOTENT_BLOCK.bit_length() - 2) + 2 * ((CHUNK // NILPOTENT_BLOCK).bit_length() - 1)
    ops = {}
    pending, n_pending, start = iter(()), 1, 0
    for size in sizes:
        group = units[start:start + size]
        start += size
        every = max(1, (size + n_products + 2) // n_pending)
        _interleave(prepare(group, ops), pending, every)
        pending, n_pending = recur(group, ops), size
    for _ in pending:
        pass
    for j in range(hpb):
        st_ref[j] = state[j]

    @pl.when(l == nblk - 1)
    def _():
        s_ref[0] = st_ref[...]


def _delta_mixer(proj, gates, conv_prev, s0, w_conv, w_norm, *, valid, rows, hpb):
    b, l, _ = proj.shape
    heads = s0.shape[1]
    nl = l // rows
    ng = heads // hpb
    width = hpb * HEAD_DIM
    blk = lambda off: pl.BlockSpec((1, rows, width), lambda bi, gi, li, off=off: (bi, li, off * ng + gi))
    prev = lambda off: pl.BlockSpec((1, CONV_W - 1, width), lambda bi, gi, li, off=off: (bi, 0, off * ng + gi))
    wcv = lambda off: pl.BlockSpec((CONV_W, width), lambda bi, gi, li, off=off: (0, off * ng + gi))
    row_spec = pl.BlockSpec((1, LANES), lambda bi, gi, li: (0, 0))
    state_spec = pl.BlockSpec((1, hpb, HEAD_DIM, HEAD_DIM), lambda bi, gi, li: (bi, gi, 0, 0))
    kern = functools.partial(_delta_kernel, rows=rows, nblk=nl, valid=valid, heads=heads, hpb=hpb)
    return pl.pallas_call(
        kern,
        grid=(b, ng, nl),
        in_specs=[blk(0), blk(1), blk(2), blk(3),
                  pl.BlockSpec((rows, LANES), lambda bi, gi, li: (bi * nl + li, 0)),
                  prev(0), prev(1), prev(2), state_spec,
                  wcv(0), wcv(1), wcv(2), row_spec],
        out_specs=[pl.BlockSpec((1, rows, width), lambda bi, gi, li: (bi, li, gi)), state_spec],
        out_shape=[jax.ShapeDtypeStruct((b, l, heads * HEAD_DIM), BF16),
                   jax.ShapeDtypeStruct(s0.shape, F32)],
        scratch_shapes=[pltpu.VMEM((rows + SUBLANES, width), F32)] * 3
        + [pltpu.VMEM((hpb, HEAD_DIM, HEAD_DIM), F32)],
        compiler_params=_cparams(("arbitrary", "arbitrary", "arbitrary")),
        name="delta_mixer",
    )(proj, proj, proj, proj, gates, conv_prev, conv_prev, conv_prev, s0,
      w_conv, w_conv, w_conv, w_norm)


def _lru_kernel(x_ref, y_ref, prev_ref, h0_ref, wc_ref, bc_ref, wa_ref, ba_ref, wx_ref, bx_ref,
                lam_ref, wn_ref, o_ref, hl_ref, ext_ref, a_ref, b_ref, h_ref, *, rows, valid, groups):
    l = pl.program_id(1)
    first = l == 0

    @pl.when(first)
    def _():
        h_ref[...] = h0_ref[0]

    xc = _conv_block(ext_ref, x_ref[0], prev_ref, wc_ref, first, rows) + bc_ref[...]
    sub = lax.broadcasted_iota(jnp.int32, (rows // SUBLANES, SUBLANES, LRU_BLOCK), 1)
    for gi in range(groups):
        cs = slice(gi * LRU_BLOCK, (gi + 1) * LRU_BLOCK)
        xg = xc[:, cs]
        xg16 = xg.astype(BF16)
        gate_r = _sigmoid(_dot(xg16, wa_ref[gi]) + ba_ref[:, cs])
        gate_i = _sigmoid(_dot(xg16, wx_ref[gi]) + bx_ref[:, cs])
        log_a = -LRU_C * gate_r * _softplus(-lam_ref[:, cs])
        a = jnp.exp(log_a)
        var = 1.0 - jnp.exp(2.0 * log_a)
        bb = (var * lax.rsqrt(jnp.maximum(var, SQRT_FLOOR))) * gate_i * xg
        a = a.reshape(rows // SUBLANES, SUBLANES, LRU_BLOCK)
        bb = bb.reshape(rows // SUBLANES, SUBLANES, LRU_BLOCK)
        for s in (1, 2, 4):
            keep = sub >= s
            a_sh = jnp.where(keep, pltpu.roll(a, s, 1), 1.0)
            b_sh = jnp.where(keep, pltpu.roll(bb, s, 1), 0.0)
            bb = a * b_sh + bb
            a = a * a_sh
        a_ref[:, cs] = a.reshape(rows, LRU_BLOCK)
        b_ref[:, cs] = bb.reshape(rows, LRU_BLOCK)

    def step(i, h_prev):
        r0 = pl.multiple_of(i * SUBLANES, SUBLANES)
        hb = a_ref[pl.ds(r0, SUBLANES), :] * h_prev + b_ref[pl.ds(r0, SUBLANES), :]
        b_ref[pl.ds(r0, SUBLANES), :] = hb
        return hb[SUBLANES - 1:SUBLANES, :]

    h_ref[...] = lax.fori_loop(0, rows // SUBLANES, step, h_ref[...])

    hs = b_ref[...]
    last_blk, last_row = divmod(valid - 1, rows)

    @pl.when(l == last_blk)
    def _():
        hl_ref[0] = hs[last_row:last_row + 1, :]

    y = _gelu_tanh(y_ref[0]) * hs
    o_ref[0] = _rms_rows(y, wn_ref[...]).astype(o_ref.dtype)


def _lru_mixer(proj, conv_prev, h0, w_conv, b_conv, w_ga, b_ga, w_gx, b_gx, lam, w_norm,
               *, valid, rows, x_blk, y_blk):
    b, l, _ = proj.shape
    d = h0.shape[-1]
    groups = d // LRU_BLOCK
    nl = l // rows
    vec = pl.BlockSpec((1, d), lambda bi, li: (0, 0))
    gw = pl.BlockSpec((groups, LRU_BLOCK, LRU_BLOCK), lambda bi, li: (0, 0, 0))
    kern = functools.partial(_lru_kernel, rows=rows, valid=valid, groups=groups)
    return pl.pallas_call(
        kern,
        grid=(b, nl),
        in_specs=[pl.BlockSpec((1, rows, d), lambda bi, li: (bi, li, x_blk)),
                  pl.BlockSpec((1, rows, d), lambda bi, li: (bi, li, y_blk)),
                  pl.BlockSpec((1, CONV_W - 1, d), lambda bi, li: (bi, 0, 0)),
                  pl.BlockSpec((1, 1, d), lambda bi, li: (bi, 0, 0)),
                  pl.BlockSpec((CONV_W, d), lambda bi, li: (0, 0)),
                  vec, gw, vec, gw, vec, vec, vec],
        out_specs=[pl.BlockSpec((1, rows, d), lambda bi, li: (bi, li, 0)),
                   pl.BlockSpec((1, 1, d), lambda bi, li: (bi, 0, 0))],
        out_shape=[jax.ShapeDtypeStruct((b, l, d), BF16), jax.ShapeDtypeStruct((b, 1, d), F32)],
        scratch_shapes=[pltpu.VMEM((rows + SUBLANES, d), F32), pltpu.VMEM((rows, d), F32),
                        pltpu.VMEM((rows, d), F32), pltpu.VMEM((1, d), F32)],
        compiler_params=_cparams(("arbitrary", "arbitrary")),
        name="lru_mixer",
    )(proj, proj, conv_prev, h0, w_conv, b_conv, w_ga, b_ga, w_gx, b_gx, lam, w_norm)


def _out_proj_kernel(x_ref, a_ref, b_ref, wa_ref, wb_ref, o_ref):
    o_ref[...] = x_ref[...] + _dot(a_ref[...], wa_ref[...]) + _dot(b_ref[...], wb_ref[...])


def _out_proj(x, mix_a, mix_b, w_out, layer, tm, tn):
    t, d = x.shape
    da, db = mix_a.shape[1], mix_b.shape[1]
    assert da == db
    return pl.pallas_call(
        _out_proj_kernel,
        grid=(t // tm, d // tn),
        in_specs=[pl.BlockSpec((tm, tn), lambda i, j: (i, j)),
                  pl.BlockSpec((tm, da), lambda i, j: (i, 0)),
                  pl.BlockSpec((tm, db), lambda i, j: (i, 0)),
                  pl.BlockSpec((None, da, tn), lambda i, j: (layer, 0, j)),
                  pl.BlockSpec((None, db, tn), lambda i, j: (layer, 1, j))],
        out_specs=pl.BlockSpec((tm, tn), lambda i, j: (i, j)),
        out_shape=jax.ShapeDtypeStruct((t, d), F32),
        compiler_params=_cparams(("arbitrary", "arbitrary")),
        name="out_proj",
    )(x, mix_a, mix_b, w_out, w_out)


def _mlp_up_kernel(x_ref, wn_ref, wu_ref, h_ref, xn_ref):
    @pl.when(pl.program_id(1) == 0)
    def _():
        xn_ref[...] = _rms_rows(x_ref[...], wn_ref[...]).astype(BF16)

    hid = jnp.maximum(_dot(xn_ref[...], wu_ref[...]), 0.0)
    h_ref[...] = (hid * hid).astype(BF16)


def _mlp_up(x, wn, w_up, layer, tm, tn):
    t, d = x.shape
    f = w_up.shape[2]
    return pl.pallas_call(
        _mlp_up_kernel,
        grid=(t // tm, f // tn),
        in_specs=[pl.BlockSpec((tm, d), lambda i, j: (i, 0)),
                  pl.BlockSpec((1, d), lambda i, j: (0, 0)),
                  pl.BlockSpec((None, d, tn), lambda i, j: (layer, 0, j))],
        out_specs=pl.BlockSpec((tm, tn), lambda i, j: (i, j)),
        out_shape=jax.ShapeDtypeStruct((t, f), BF16),
        scratch_shapes=[pltpu.VMEM((tm, d), BF16)],
        compiler_params=_cparams(("arbitrary", "arbitrary")),
        name="mlp_up",
    )(x, wn, w_up)


def _mlp_down_kernel(x_ref, h_ref, wd_ref, o_ref):
    @pl.when(pl.program_id(2) == 0)
    def _():
        o_ref[...] = x_ref[...]

    o_ref[...] += _dot(h_ref[...], wd_ref[...])


def _mlp_down(x, hid, w_down, layer, tm, tn, tk):
    t, d = x.shape
    f = hid.shape[1]
    return pl.pallas_call(
        _mlp_down_kernel,
        grid=(t // tm, d // tn, f // tk),
        in_specs=[pl.BlockSpec((tm, tn), lambda i, j, k: (i, j)),
                  pl.BlockSpec((tm, tk), lambda i, j, k: (i, k)),
                  pl.BlockSpec((None, tk, tn), lambda i, j, k: (layer, k, j))],
        out_specs=pl.BlockSpec((tm, tn), lambda i, j, k: (i, j)),
        out_shape=jax.ShapeDtypeStruct((t, d), F32),
        compiler_params=_cparams(("arbitrary", "arbitrary", "arbitrary")),
        name="mlp_down",
    )(x, hid, w_down)


def _final_norm_kernel(x_ref, w_ref, o_ref):
    o_ref[...] = _rms_rows(x_ref[...], w_ref[...])


def _final_norm(x, w, tm):
    t, d = x.shape
    return pl.pallas_call(
        _final_norm_kernel,
        grid=(t // tm,),
        in_specs=[pl.BlockSpec((tm, d), lambda i: (i, 0)), pl.BlockSpec((1, d), lambda i: (0, 0))],
        out_specs=pl.BlockSpec((tm, d), lambda i: (i, 0)),
        out_shape=jax.ShapeDtypeStruct((t, d), F32),
        compiler_params=_cparams(("arbitrary",)),
        name="final_norm",
    )(x, w)


def _tile(n, pref):
    t = min(n, pref)
    while n % t:
        t //= 2
    return t


def _prep_stacked(p):
    d_a = p["w_conv_delta"].shape[-1] // 3
    heads = p["a_log"].shape[-1]
    w_in = p["w_in"]
    gate_lo, gate_hi = 4 * d_a, 4 * d_a + 2 * heads
    assert gate_lo % LANES == 0 and gate_lo + LANES <= w_in.shape[2] and 2 * heads <= LANES
    return dict(
        w_main=jnp.concatenate([w_in[:, :, :gate_lo], w_in[:, :, gate_hi:]], axis=2).astype(BF16),
        w_gate=w_in[:, :, gate_lo:gate_lo + LANES].astype(BF16),
        w_out=p["w_out"].astype(BF16), w_mlp_up=p["w_mlp_up"].astype(BF16),
        w_mlp_down=p["w_mlp_down"].astype(BF16))


def _prep_layer(i, p, stacked):
    heads = p["a_log"].shape[-1]
    row = lambda v: jnp.pad(v, (heads, LANES - 2 * heads))[None, :].astype(F32)
    return dict(
        stacked, layer=i,
        w_norm_mix=p["w_norm_mix"][i][None, :],
        w_conv_delta=p["w_conv_delta"][i], a_row=row(p["a_log"][i]), dt_row=row(p["dt_bias"][i]),
        w_norm_delta=p["w_norm_delta"][i][None, :],
        w_conv_lru=p["w_conv_lru"][i], b_conv_lru=p["b_conv_lru"][i][None, :],
        w_gate_a=p["w_gate_a"][i].astype(BF16), b_gate_a=p["b_gate_a"][i][None, :],
        w_gate_x=p["w_gate_x"][i].astype(BF16), b_gate_x=p["b_gate_x"][i][None, :],
        lam=p["lam"][i][None, :], w_norm_lru=p["w_norm_lru"][i][None, :],
        w_norm_mlp=p["w_norm_mlp"][i][None, :],
    )


def _layer(x, conv_d, s_d, conv_l, h_l, p, w_final, final_norm):
    b, l, d = x.shape
    t = b * l
    d_a = p["w_conv_delta"].shape[-1] // 3
    d_b = p["w_conv_lru"].shape[-1]
    assert d_a == d_b and d_a % HEAD_DIM == 0
    layer = p["layer"]
    tm = _tile(t, 512)
    xf = x.reshape(t, d)

    proj, gates = _norm_proj(xf, p["w_norm_mix"], p["w_main"], p["w_gate"], p["a_row"], p["dt_row"], layer,
                             d_a // HEAD_DIM, tm, _tile(p["w_main"].shape[2], 1536))
    n_main = proj.shape[1]
    proj = proj.reshape(b, l, n_main)
    conv_d_new = jnp.concatenate([conv_d, proj[:, :, :3 * d_a]], axis=1)[:, -(CONV_W - 1):]
    conv_l_new = jnp.concatenate([conv_l, proj[:, :, 4 * d_a:4 * d_a + d_b]], axis=1)[:, -(CONV_W - 1):]

    lp = -(-l // CHUNK) * CHUNK
    if lp != l:
        proj = jnp.pad(proj, ((0, 0), (0, lp - l), (0, 0)))
        gates = jnp.pad(gates.reshape(b, l, LANES), ((0, 0), (0, lp - l), (0, 0))).reshape(b * lp, LANES)

    delta_rows = _tile(lp, DELTA_UNITS * CHUNK)
    delta_out, s_d_new = _delta_mixer(
        proj, gates, conv_d, s_d, p["w_conv_delta"], p["w_norm_delta"],
        valid=l, rows=delta_rows, hpb=_tile(d_a // HEAD_DIM, DELTA_UNITS * CHUNK // delta_rows))
    lru_out, h_new = _lru_mixer(
        proj, conv_l, h_l[:, None, :], p["w_conv_lru"], p["b_conv_lru"], p["w_gate_a"], p["b_gate_a"],
        p["w_gate_x"], p["b_gate_x"], p["lam"], p["w_norm_lru"],
        valid=l, rows=_tile(lp, 256), x_blk=4 * d_a // d_b, y_blk=4 * d_a // d_b + 1)
    if lp != l:
        delta_out, lru_out = delta_out[:, :l], lru_out[:, :l]

    x1 = _out_proj(xf, delta_out.reshape(t, d_a), lru_out.reshape(t, d_b), p["w_out"], layer,
                   _tile(t, 1024), _tile(d, 1024))
    f = p["w_mlp_up"].shape[2]
    hid = _mlp_up(x1, p["w_norm_mlp"], p["w_mlp_up"], layer, tm, _tile(f, 1024))
    x2 = _mlp_down(x1, hid, p["w_mlp_down"], layer, _tile(t, 1024), _tile(d, 1024), _tile(f, 4096))
    if final_norm:
        x2 = _final_norm(x2, w_final, tm)
    return x2.reshape(b, l, d), s_d_new, conv_d_new, h_new[:, 0, :], conv_l_new


def _trunk(x, s_d, conv_d, h_l, conv_l, layers, w_final):
    sds, cds, hls, cls = [], [], [], []
    depth = len(layers)
    for i, p in enumerate(layers):
        x, sd, cd, hl, cl = _layer(x, conv_d[i], s_d[i], conv_l[i], h_l[i], p, w_final, i == depth - 1)
        sds.append(sd)
        cds.append(cd)
        hls.append(hl)
        cls.append(cl)
    return x, jnp.stack(sds), jnp.stack(cds), jnp.stack(hls), jnp.stack(cls)


def kernel(x_prompt, x_sample, state_delta, state_conv_delta, state_lru, state_conv_lru,
           w_norm_mix, w_in, w_conv_delta, a_log, dt_bias, w_norm_delta,
           w_conv_lru, b_conv_lru, w_gate_a, b_gate_a, w_gate_x, b_gate_x,
           lam, w_norm_lru, w_out, w_norm_mlp, w_mlp_up, w_mlp_down, w_norm_final):
    params = dict(w_norm_mix=w_norm_mix, w_in=w_in, w_conv_delta=w_conv_delta, a_log=a_log, dt_bias=dt_bias,
                  w_norm_delta=w_norm_delta, w_conv_lru=w_conv_lru, b_conv_lru=b_conv_lru,
                  w_gate_a=w_gate_a, b_gate_a=b_gate_a, w_gate_x=w_gate_x, b_gate_x=b_gate_x,
                  lam=lam, w_norm_lru=w_norm_lru, w_out=w_out, w_norm_mlp=w_norm_mlp,
                  w_mlp_up=w_mlp_up, w_mlp_down=w_mlp_down)
    depth = w_in.shape[0]
    stacked = _prep_stacked(params)
    layers = [_prep_layer(i, params, stacked) for i in range(depth)]
    w_final = w_norm_final[None, :]

    bp = x_prompt.shape[0]
    z_sd = jnp.zeros((depth, bp) + state_delta.shape[2:], F32)
    z_cd = jnp.zeros((depth, bp) + state_conv_delta.shape[2:], x_prompt.dtype)
    z_hl = jnp.zeros((depth, bp) + state_lru.shape[2:], F32)
    z_cl = jnp.zeros((depth, bp) + state_conv_lru.shape[2:], x_prompt.dtype)
    outs_p = _trunk(x_prompt, z_sd, z_cd, z_hl, z_cl, layers, w_final)
    outs_s = _trunk(x_sample, state_delta, state_conv_delta, state_lru, state_conv_lru, layers, w_final)
    return (outs_p[0], outs_s[0]) + outs_p[1:] + outs_s[1:]
```

```python
import functools

import jax
import jax.numpy as jnp
from jax import lax
from jax.experimental import pallas as pl
from jax.experimental.pallas import tpu as pltpu

HEAD_DIM = 128
LRU_BLOCK = 128
CONV_W = 4
LRU_C = 8.0
EPS = 1e-6
LANES = 128
SUBLANES = 8
CHUNK = 128
NILPOTENT_BLOCK = 16
GROUP_CHUNKS = 4
DELTA_UNITS = 16
SQRT_FLOOR = 1e-30
VMEM_LIMIT = 60 * 1024 * 1024

F32 = jnp.float32
BF16 = jnp.bfloat16


def _cparams(sem):
    return pltpu.CompilerParams(dimension_semantics=sem, vmem_limit_bytes=VMEM_LIMIT)


def _dot(a, b):
    return jnp.dot(a, b, preferred_element_type=F32)


def _dot_nt(a, b):
    return lax.dot_general(a, b, (((1,), (1,)), ((), ())), preferred_element_type=F32)


def _split_bf16(x):
    hi = x.astype(BF16)
    lo = (x - hi.astype(F32)).astype(BF16)
    return hi, lo


def _dot3(xs, ys):
    (xh, xl), (yh, yl) = xs, ys
    lhs = jnp.concatenate([xh, xl, xh], axis=1)
    rhs = jnp.concatenate([yh, yh, yl], axis=0)
    return _dot(lhs, rhs)


def _cumsum_rows(tril16, x):
    hi, lo = _split_bf16(x)
    lo2 = (x - hi.astype(F32) - lo.astype(F32)).astype(BF16)
    lhs = jnp.concatenate([tril16, tril16, tril16], axis=1)
    rhs = jnp.concatenate([hi, lo, lo2], axis=0)
    return _dot(lhs, rhs)


def _sigmoid(x):
    return 0.5 * jnp.tanh(0.5 * x) + 0.5


def _softplus(x):
    return jnp.maximum(x, 0.0) + jnp.log1p(jnp.exp(-jnp.abs(x)))


def _silu(x):
    return x * _sigmoid(x)


def _gelu_tanh(x):
    c = 0.7978845608028654
    return 0.5 * x * (1.0 + jnp.tanh(c * (x + 0.044715 * (x * x * x))))


def _rms_rows(x, w):
    ms = jnp.mean(x * x, axis=-1, keepdims=True)
    return (x * lax.rsqrt(ms + EPS)) * w


def _norm_proj_kernel(x_ref, wn_ref, w_ref, wg_ref, arow_ref, dtrow_ref, o_ref, g_ref, xn_ref, *, heads):
    @pl.when(pl.program_id(1) == 0)
    def _():
        xn_ref[...] = _rms_rows(x_ref[...], wn_ref[...]).astype(BF16)
        logits = _dot(xn_ref[...], wg_ref[...])
        lane = lax.broadcasted_iota(jnp.int32, logits.shape, 1)
        log_decay = -jnp.exp(arow_ref[...]) * _softplus(logits + dtrow_ref[...])
        g_ref[...] = jnp.where(lane < heads, _sigmoid(logits), log_decay)

    o_ref[...] = _dot(xn_ref[...], w_ref[...])


def _norm_proj(x, wn, w_main, w_gate, a_row, dt_row, layer, heads, tm, tn):
    t, d = x.shape
    n = w_main.shape[2]
    row_spec = pl.BlockSpec((1, LANES), lambda i, j: (0, 0))
    return pl.pallas_call(
        functools.partial(_norm_proj_kernel, heads=heads),
        grid=(t // tm, n // tn),
        in_specs=[
            pl.BlockSpec((tm, d), lambda i, j: (i, 0)),
            pl.BlockSpec((1, d), lambda i, j: (0, 0)),
            pl.BlockSpec((None, d, tn), lambda i, j: (layer, 0, j)),
            pl.BlockSpec((None, d, LANES), lambda i, j: (layer, 0, 0)),
            row_spec, row_spec,
        ],
        out_specs=[
            pl.BlockSpec((tm, tn), lambda i, j: (i, j)),
            pl.BlockSpec((tm, LANES), lambda i, j: (i, 0)),
        ],
        out_shape=[jax.ShapeDtypeStruct((t, n), F32), jax.ShapeDtypeStruct((t, LANES), F32)],
        scratch_shapes=[pltpu.VMEM((tm, d), BF16)],
        compiler_params=_cparams(("arbitrary", "arbitrary")),
        name="norm_proj",
    )(x, wn, w_main, w_gate, a_row, dt_row)


def _conv_block(ext_ref, x, prev_ref, w_ref, first, rows):
    lo = SUBLANES - (CONV_W - 1)

    @pl.when(first)
    def _():
        ext_ref[lo:SUBLANES, :] = prev_ref[0]

    @pl.when(jnp.logical_not(first))
    def _():
        ext_ref[lo:SUBLANES, :] = ext_ref[rows + lo:rows + SUBLANES, :]

    ext_ref[SUBLANES:SUBLANES + rows, :] = x
    y = ext_ref[lo:lo + rows, :] * w_ref[0:1, :]
    for j in range(1, CONV_W):
        y = y + ext_ref[lo + j:lo + j + rows, :] * w_ref[j:j + 1, :]
    return y


def _unit_lower_inverse_stages(mats, row, col, live, out):
    n = min(mats[0].shape[0], live)
    eye = (row == col).astype(F32)
    shift = NILPOTENT_BLOCK.bit_length() - 1
    same = (row >> shift) == (col >> shift)
    ps = [jnp.where(same, a, 0.0) for a in mats]
    ts = [eye - p for p in ps]
    pss = [_split_bf16(p) for p in ps]
    size = 2
    while size < NILPOTENT_BLOCK:
        pss = [_split_bf16(_dot3(p, p)) for p in pss]
        yield
        ts = [t + _dot3(_split_bf16(t), p) for t, p in zip(ts, pss)]
        yield
        size *= 2
    blk = NILPOTENT_BLOCK
    tss = [_split_bf16(t) for t in ts]
    while blk < n:
        shift += 1
        nxt = (row >> shift) == (col >> shift)
        sel = jnp.logical_and(nxt, jnp.logical_not(same))
        nss = [_split_bf16(_dot3(t, _split_bf16(jnp.where(sel, a, 0.0)))) for t, a in zip(tss, mats)]
        yield
        ts = [t - _dot3(nm, t2) for t, nm, t2 in zip(ts, nss, tss)]
        tss = [_split_bf16(t) for t in ts]
        yield
        same = nxt
        blk *= 2
    out.extend(tss)


def _interleave(main, side, every):
    for count, _ in enumerate(main, 1):
        if count % every == 0:
            next(side, None)
    for _ in side:
        pass


def _delta_kernel(q_ref, k_ref, v_ref, z_ref, g_ref, pq_ref, pk_ref, pv_ref, s0_ref,
                  wq_ref, wk_ref, wv_ref, wnorm_ref,
                  o_ref, s_ref, eq_ref, ek_ref, ev_ref, st_ref, *, rows, nblk, valid, heads, hpb):
    head0 = pl.program_id(1) * hpb
    l = pl.program_id(2)
    first = l == 0

    @pl.when(first)
    def _():
        st_ref[...] = s0_ref[0]

    q_all = _silu(_conv_block(eq_ref, q_ref[0], pq_ref, wq_ref, first, rows))
    k_all = _silu(_conv_block(ek_ref, k_ref[0], pk_ref, wk_ref, first, rows))
    v_all = _silu(_conv_block(ev_ref, v_ref[0], pv_ref, wv_ref, first, rows))
    zg = _silu(z_ref[0])

    gates = g_ref[...]
    lane = lax.broadcasted_iota(jnp.int32, gates.shape, 1)
    if valid < rows * nblk:
        live_rows = l * rows + lax.broadcasted_iota(jnp.int32, (rows, 1), 0) < valid
    per_head = []
    for j in range(hpb):
        hs = slice(j * HEAD_DIM, (j + 1) * HEAD_DIM)
        q, k = q_all[:, hs], k_all[:, hs]
        q = q * lax.rsqrt(jnp.sum(q * q, axis=-1, keepdims=True) + EPS) * (HEAD_DIM ** -0.5)
        k = k * lax.rsqrt(jnp.sum(k * k, axis=-1, keepdims=True) + EPS)
        beta = jnp.sum(jnp.where(lane == head0 + j, gates, 0.0), axis=1, keepdims=True)
        g = jnp.sum(jnp.where(lane == heads + head0 + j, gates, 0.0), axis=1, keepdims=True)
        if valid < rows * nblk:
            beta = jnp.where(live_rows, beta, 0.0)
            g = jnp.where(live_rows, g, 0.0)
            k = jnp.where(live_rows, k, 0.0)
        per_head.append((q, k, v_all[:, hs], beta, g))

    row = lax.broadcasted_iota(jnp.int32, (CHUNK, CHUNK), 0)
    col = lax.broadcasted_iota(jnp.int32, (CHUNK, CHUNK), 1)
    causal = row >= col
    strict = row > col
    tril16 = causal.astype(BF16)

    live = min(CHUNK, valid) if nblk * rows == CHUNK else CHUNK

    def prepare(units, out):
        pre = []
        for j, c in units:
            q, k, v, beta, g = per_head[j]
            sl = slice(c * CHUNK, (c + 1) * CHUNK)
            qc, kc, vc = q[sl], k[sl], v[sl]
            beta_b = jnp.broadcast_to(beta[sl], (CHUNK, CHUNK))
            g_b = jnp.broadcast_to(g[sl], (CHUNK, CHUNK))
            gcum = _cumsum_rows(tril16, g_b)
            decay = jnp.where(causal, jnp.exp(jnp.where(causal, gcum - gcum.T, 0.0)), 0.0)
            eg = jnp.exp(gcum)
            kb = kc * beta_b
            kc16 = kc.astype(BF16)
            a_mat = jnp.where(strict, _dot_nt(kb.astype(BF16), kc16) * decay, 0.0)
            g_last = gcum[CHUNK - 1:CHUNK, :]
            pre.append(dict(
                a=a_mat, rhs=jnp.concatenate([vc * beta_b, kb * eg], axis=1),
                qk=(_dot_nt(qc.astype(BF16), kc16) * decay).astype(BF16), qg=qc * eg,
                kdt=(kc * jnp.exp(g_last - gcum)).T.astype(BF16), gl=jnp.exp(g_last)))
            yield
        t_invs = []
        yield from _unit_lower_inverse_stages([p["a"] for p in pre], row, col, live, t_invs)
        sols = [_dot3(t, _split_bf16(p["rhs"])).astype(BF16) for t, p in zip(t_invs, pre)]
        yield
        for unit, p, sol in zip(units, pre, sols):
            ks = _dot(p["kdt"], sol)
            qs = _dot(p["qk"], sol)
            lhs = jnp.concatenate([p["qg"] - qs[:, HEAD_DIM:], ks[:, HEAD_DIM:]], axis=0).astype(BF16)
            out[unit] = (lhs, qs[:, :HEAD_DIM], ks[:, :HEAD_DIM], p["gl"])
        yield

    state = [st_ref[j] for j in range(hpb)]

    def recur(units, ops):
        for j, c in units:
            sl = slice(c * CHUNK, (c + 1) * CHUNK)
            hs = slice(j * HEAD_DIM, (j + 1) * HEAD_DIM)
            lhs, o0, s_add, gl = ops[(j, c)]
            s = state[j]
            both = _dot(lhs, s.astype(BF16))
            o = both[:CHUNK] + o0
            state[j] = (s * gl - both[CHUNK:]) + s_add
            o_ref[0, sl, hs] = (_rms_rows(o, wnorm_ref[...]) * zg[sl, hs]).astype(o_ref.dtype)
            yield

    units = [(j, c) for j in range(hpb) for c in range(rows // CHUNK)]
    n = len(units)
    sizes = [n] if n <= GROUP_CHUNKS else [n // 2] + [GROUP_CHUNKS] * (n // 2 // GROUP_CHUNKS)
    n_products = 2 * (NILPOTENT_BLOCK.bit_length() - 2) + 2 * ((CHUNK // NILPOTENT_BLOCK).bit_length() - 1)
    ops = {}
    pending, n_pending, start = iter(()), 1, 0
    for size in sizes:
        group = units[start:start + size]
        start += size
        every = max(1, (size + n_products + 2) // n_pending)
        _interleave(prepare(group, ops), pending, every)
        pending, n_pending = recur(group, ops), size
    for _ in pending:
        pass
    for j in range(hpb):
        st_ref[j] = state[j]

    @pl.when(l == nblk - 1)
    def _():
        s_ref[0] = st_ref[...]


def _delta_mixer(proj, gates, conv_prev, s0, w_conv, w_norm, *, valid, rows, hpb):
    b, l, _ = proj.shape
    heads = s0.shape[1]
    nl = l // rows
    ng = heads // hpb
    width = hpb * HEAD_DIM
    blk = lambda off: pl.BlockSpec((1, rows, width), lambda bi, gi, li, off=off: (bi, li, off * ng + gi))
    prev = lambda off: pl.BlockSpec((1, CONV_W - 1, width), lambda bi, gi, li, off=off: (bi, 0, off * ng + gi))
    wcv = lambda off: pl.BlockSpec((CONV_W, width), lambda bi, gi, li, off=off: (0, off * ng + gi))
    row_spec = pl.BlockSpec((1, LANES), lambda bi, gi, li: (0, 0))
    state_spec = pl.BlockSpec((1, hpb, HEAD_DIM, HEAD_DIM), lambda bi, gi, li: (bi, gi, 0, 0))
    kern = functools.partial(_delta_kernel, rows=rows, nblk=nl, valid=valid, heads=heads, hpb=hpb)
    return pl.pallas_call(
        kern,
        grid=(b, ng, nl),
        in_specs=[blk(0), blk(1), blk(2), blk(3),
                  pl.BlockSpec((rows, LANES), lambda bi, gi, li: (bi * nl + li, 0)),
                  prev(0), prev(1), prev(2), state_spec,
                  wcv(0), wcv(1), wcv(2), row_spec],
        out_specs=[pl.BlockSpec((1, rows, width), lambda bi, gi, li: (bi, li, gi)), state_spec],
        out_shape=[jax.ShapeDtypeStruct((b, l, heads * HEAD_DIM), BF16),
                   jax.ShapeDtypeStruct(s0.shape, F32)],
        scratch_shapes=[pltpu.VMEM((rows + SUBLANES, width), F32)] * 3
        + [pltpu.VMEM((hpb, HEAD_DIM, HEAD_DIM), F32)],
        compiler_params=_cparams(("arbitrary", "arbitrary", "arbitrary")),
        name="delta_mixer",
    )(proj, proj, proj, proj, gates, conv_prev, conv_prev, conv_prev, s0,
      w_conv, w_conv, w_conv, w_norm)


def _lru_kernel(x_ref, y_ref, prev_ref, h0_ref, wc_ref, bc_ref, wa_ref, ba_ref, wx_ref, bx_ref,
                lam_ref, wn_ref, o_ref, hl_ref, ext_ref, a_ref, b_ref, h_ref, *, rows, valid, groups):
    l = pl.program_id(1)
    first = l == 0

    @pl.when(first)
    def _():
        h_ref[...] = h0_ref[0]

    xc = _conv_block(ext_ref, x_ref[0], prev_ref, wc_ref, first, rows) + bc_ref[...]
    sub = lax.broadcasted_iota(jnp.int32, (rows // SUBLANES, SUBLANES, LRU_BLOCK), 1)
    for gi in range(groups):
        cs = slice(gi * LRU_BLOCK, (gi + 1) * LRU_BLOCK)
        xg = xc[:, cs]
        xg16 = xg.astype(BF16)
        gate_r = _sigmoid(_dot(xg16, wa_ref[gi]) + ba_ref[:, cs])
        gate_i = _sigmoid(_dot(xg16, wx_ref[gi]) + bx_ref[:, cs])
        log_a = -LRU_C * gate_r * _softplus(-lam_ref[:, cs])
        a = jnp.exp(log_a)
        var = 1.0 - jnp.exp(2.0 * log_a)
        bb = (var * lax.rsqrt(jnp.maximum(var, SQRT_FLOOR))) * gate_i * xg
        a = a.reshape(rows // SUBLANES, SUBLANES, LRU_BLOCK)
        bb = bb.reshape(rows // SUBLANES, SUBLANES, LRU_BLOCK)
        for s in (1, 2, 4):
            keep = sub >= s
            a_sh = jnp.where(keep, pltpu.roll(a, s, 1), 1.0)
            b_sh = jnp.where(keep, pltpu.roll(bb, s, 1), 0.0)
            bb = a * b_sh + bb
            a = a * a_sh
        a_ref[:, cs] = a.reshape(rows, LRU_BLOCK)
        b_ref[:, cs] = bb.reshape(rows, LRU_BLOCK)

    def step(i, h_prev):
        r0 = pl.multiple_of(i * SUBLANES, SUBLANES)
        hb = a_ref[pl.ds(r0, SUBLANES), :] * h_prev + b_ref[pl.ds(r0, SUBLANES), :]
        b_ref[pl.ds(r0, SUBLANES), :] = hb
        return hb[SUBLANES - 1:SUBLANES, :]

    h_ref[...] = lax.fori_loop(0, rows // SUBLANES, step, h_ref[...])

    hs = b_ref[...]
    last_blk, last_row = divmod(valid - 1, rows)

    @pl.when(l == last_blk)
    def _():
        hl_ref[0] = hs[last_row:last_row + 1, :]

    y = _gelu_tanh(y_ref[0]) * hs
    o_ref[0] = _rms_rows(y, wn_ref[...]).astype(o_ref.dtype)


def _lru_mixer(proj, conv_prev, h0, w_conv, b_conv, w_ga, b_ga, w_gx, b_gx, lam, w_norm,
               *, valid, rows, x_blk, y_blk):
    b, l, _ = proj.shape
    d = h0.shape[-1]
    groups = d // LRU_BLOCK
    nl = l // rows
    vec = pl.BlockSpec((1, d), lambda bi, li: (0, 0))
    gw = pl.BlockSpec((groups, LRU_BLOCK, LRU_BLOCK), lambda bi, li: (0, 0, 0))
    kern = functools.partial(_lru_kernel, rows=rows, valid=valid, groups=groups)
    return pl.pallas_call(
        kern,
        grid=(b, nl),
        in_specs=[pl.BlockSpec((1, rows, d), lambda bi, li: (bi, li, x_blk)),
                  pl.BlockSpec((1, rows, d), lambda bi, li: (bi, li, y_blk)),
                  pl.BlockSpec((1, CONV_W - 1, d), lambda bi, li: (bi, 0, 0)),
                  pl.BlockSpec((1, 1, d), lambda bi, li: (bi, 0, 0)),
                  pl.BlockSpec((CONV_W, d), lambda bi, li: (0, 0)),
                  vec, gw, vec, gw, vec, vec, vec],
        out_specs=[pl.BlockSpec((1, rows, d), lambda bi, li: (bi, li, 0)),
                   pl.BlockSpec((1, 1, d), lambda bi, li: (bi, 0, 0))],
        out_shape=[jax.ShapeDtypeStruct((b, l, d), BF16), jax.ShapeDtypeStruct((b, 1, d), F32)],
        scratch_shapes=[pltpu.VMEM((rows + SUBLANES, d), F32), pltpu.VMEM((rows, d), F32),
                        pltpu.VMEM((rows, d), F32), pltpu.VMEM((1, d), F32)],
        compiler_params=_cparams(("arbitrary", "arbitrary")),
        name="lru_mixer",
    )(proj, proj, conv_prev, h0, w_conv, b_conv, w_ga, b_ga, w_gx, b_gx, lam, w_norm)


def _out_proj_kernel(x_ref, a_ref, b_ref, wa_ref, wb_ref, o_ref):
    o_ref[...] = x_ref[...] + _dot(a_ref[...], wa_ref[...]) + _dot(b_ref[...], wb_ref[...])


def _out_proj(x, mix_a, mix_b, w_out, layer, tm, tn):
    t, d = x.shape
    da, db = mix_a.shape[1], mix_b.shape[1]
    assert da == db
    return pl.pallas_call(
        _out_proj_kernel,
        grid=(t // tm, d // tn),
        in_specs=[pl.BlockSpec((tm, tn), lambda i, j: (i, j)),
                  pl.BlockSpec((tm, da), lambda i, j: (i, 0)),
                  pl.BlockSpec((tm, db), lambda i, j: (i, 0)),
                  pl.BlockSpec((None, da, tn), lambda i, j: (layer, 0, j)),
                  pl.BlockSpec((None, db, tn), lambda i, j: (layer, 1, j))],
        out_specs=pl.BlockSpec((tm, tn), lambda i, j: (i, j)),
        out_shape=jax.ShapeDtypeStruct((t, d), F32),
        compiler_params=_cparams(("arbitrary", "arbitrary")),
        name="out_proj",
    )(x, mix_a, mix_b, w_out, w_out)


def _mlp_up_kernel(x_ref, wn_ref, wu_ref, h_ref, xn_ref):
    @pl.when(pl.program_id(1) == 0)
    def _():
        xn_ref[...] = _rms_rows(x_ref[...], wn_ref[...]).astype(BF16)

    hid = jnp.maximum(_dot(xn_ref[...], wu_ref[...]), 0.0)
    h_ref[...] = (hid * hid).astype(BF16)


def _mlp_up(x, wn, w_up, layer, tm, tn):
    t, d = x.shape
    f = w_up.shape[2]
    return pl.pallas_call(
        _mlp_up_kernel,
        grid=(t // tm, f // tn),
        in_specs=[pl.BlockSpec((tm, d), lambda i, j: (i, 0)),
                  pl.BlockSpec((1, d), lambda i, j: (0, 0)),
                  pl.BlockSpec((None, d, tn), lambda i, j: (layer, 0, j))],
        out_specs=pl.BlockSpec((tm, tn), lambda i, j: (i, j)),
        out_shape=jax.ShapeDtypeStruct((t, f), BF16),
        scratch_shapes=[pltpu.VMEM((tm, d), BF16)],
        compiler_params=_cparams(("arbitrary", "arbitrary")),
        name="mlp_up",
    )(x, wn, w_up)


def _mlp_down_kernel(x_ref, h_ref, wd_ref, o_ref):
    @pl.when(pl.program_id(2) == 0)
    def _():
        o_ref[...] = x_ref[...]

    o_ref[...] += _dot(h_ref[...], wd_ref[...])


def _mlp_down(x, hid, w_down, layer, tm, tn, tk):
    t, d = x.shape
    f = hid.shape[1]
    return pl.pallas_call(
        _mlp_down_kernel,
        grid=(t // tm, d // tn, f // tk),
        in_specs=[pl.BlockSpec((tm, tn), lambda i, j, k: (i, j)),
                  pl.BlockSpec((tm, tk), lambda i, j, k: (i, k)),
                  pl.BlockSpec((None, tk, tn), lambda i, j, k: (layer, k, j))],
        out_specs=pl.BlockSpec((tm, tn), lambda i, j, k: (i, j)),
        out_shape=jax.ShapeDtypeStruct((t, d), F32),
        compiler_params=_cparams(("arbitrary", "arbitrary", "arbitrary")),
        name="mlp_down",
    )(x, hid, w_down)


def _final_norm_kernel(x_ref, w_ref, o_ref):
    o_ref[...] = _rms_rows(x_ref[...], w_ref[...])


def _final_norm(x, w, tm):
    t, d = x.shape
    return pl.pallas_call(
        _final_norm_kernel,
        grid=(t // tm,),
        in_specs=[pl.BlockSpec((tm, d), lambda i: (i, 0)), pl.BlockSpec((1, d), lambda i: (0, 0))],
        out_specs=pl.BlockSpec((tm, d), lambda i: (i, 0)),
        out_shape=jax.ShapeDtypeStruct((t, d), F32),
        compiler_params=_cparams(("arbitrary",)),
        name="final_norm",
    )(x, w)


def _tile(n, pref):
    t = min(n, pref)
    while n % t:
        t //= 2
    return t


def _regroup_kernel(w_ref, o_ref, g_ref, *, gate_lo, gate_hi):
    o_ref[:, :gate_lo] = w_ref[:, :gate_lo].astype(BF16)
    o_ref[:, gate_lo:] = w_ref[:, gate_hi:].astype(BF16)
    g_ref[...] = w_ref[:, gate_lo:gate_lo + LANES].astype(BF16)


def _regroup_w_in(w_in, gate_lo, gate_hi, tr):
    depth, d, n_in = w_in.shape
    n_main = n_in - (gate_hi - gate_lo)
    return pl.pallas_call(
        functools.partial(_regroup_kernel, gate_lo=gate_lo, gate_hi=gate_hi),
        grid=(depth, d // tr),
        in_specs=[pl.BlockSpec((None, tr, n_in), lambda a, r: (a, r, 0))],
        out_specs=[pl.BlockSpec((None, tr, n_main), lambda a, r: (a, r, 0)),
                   pl.BlockSpec((None, tr, LANES), lambda a, r: (a, r, 0))],
        out_shape=[jax.ShapeDtypeStruct((depth, d, n_main), BF16),
                   jax.ShapeDtypeStruct((depth, d, LANES), BF16)],
        compiler_params=_cparams(("arbitrary", "arbitrary")),
        name="regroup_w_in",
    )(w_in)


def _prep_stacked(p):
    d_a = p["w_conv_delta"].shape[-1] // 3
    heads = p["a_log"].shape[-1]
    w_in = p["w_in"]
    gate_lo, gate_hi = 4 * d_a, 4 * d_a + 2 * heads
    assert gate_lo % LANES == 0 and gate_lo + LANES <= w_in.shape[2] and 2 * heads <= LANES
    w_main, w_gate = _regroup_w_in(w_in, gate_lo, gate_hi, _tile(w_in.shape[1], 256))
    return dict(
        w_main=w_main, w_gate=w_gate,
        w_out=p["w_out"].astype(BF16), w_mlp_up=p["w_mlp_up"].astype(BF16),
        w_mlp_down=p["w_mlp_down"].astype(BF16))


def _prep_layer(i, p, stacked):
    heads = p["a_log"].shape[-1]
    row = lambda v: jnp.pad(v, (heads, LANES - 2 * heads))[None, :].astype(F32)
    return dict(
        stacked, layer=i,
        w_norm_mix=p["w_norm_mix"][i][None, :],
        w_conv_delta=p["w_conv_delta"][i], a_row=row(p["a_log"][i]), dt_row=row(p["dt_bias"][i]),
        w_norm_delta=p["w_norm_delta"][i][None, :],
        w_conv_lru=p["w_conv_lru"][i], b_conv_lru=p["b_conv_lru"][i][None, :],
        w_gate_a=p["w_gate_a"][i].astype(BF16), b_gate_a=p["b_gate_a"][i][None, :],
        w_gate_x=p["w_gate_x"][i].astype(BF16), b_gate_x=p["b_gate_x"][i][None, :],
        lam=p["lam"][i][None, :], w_norm_lru=p["w_norm_lru"][i][None, :],
        w_norm_mlp=p["w_norm_mlp"][i][None, :],
    )


def _layer(x, conv_d, s_d, conv_l, h_l, p, w_final, final_norm):
    b, l, d = x.shape
    t = b * l
    d_a = p["w_conv_delta"].shape[-1] // 3
    d_b = p["w_conv_lru"].shape[-1]
    assert d_a == d_b and d_a % HEAD_DIM == 0
    layer = p["layer"]
    tm = _tile(t, 512)
    xf = x.reshape(t, d)

    proj, gates = _norm_proj(xf, p["w_norm_mix"], p["w_main"], p["w_gate"], p["a_row"], p["dt_row"], layer,
                             d_a // HEAD_DIM, tm, _tile(p["w_main"].shape[2], 1536))
    n_main = proj.shape[1]
    proj = proj.reshape(b, l, n_main)
    conv_d_new = jnp.concatenate([conv_d, proj[:, :, :3 * d_a]], axis=1)[:, -(CONV_W - 1):]
    conv_l_new = jnp.concatenate([conv_l, proj[:, :, 4 * d_a:4 * d_a + d_b]], axis=1)[:, -(CONV_W - 1):]

    lp = -(-l // CHUNK) * CHUNK
    if lp != l:
        proj = jnp.pad(proj, ((0, 0), (0, lp - l), (0, 0)))
        gates = jnp.pad(gates.reshape(b, l, LANES), ((0, 0), (0, lp - l), (0, 0))).reshape(b * lp, LANES)

    delta_rows = _tile(lp, DELTA_UNITS * CHUNK)
    delta_out, s_d_new = _delta_mixer(
        proj, gates, conv_d, s_d, p["w_conv_delta"], p["w_norm_delta"],
        valid=l, rows=delta_rows, hpb=_tile(d_a // HEAD_DIM, DELTA_UNITS * CHUNK // delta_rows))
    lru_out, h_new = _lru_mixer(
        proj, conv_l, h_l[:, None, :], p["w_conv_lru"], p["b_conv_lru"], p["w_gate_a"], p["b_gate_a"],
        p["w_gate_x"], p["b_gate_x"], p["lam"], p["w_norm_lru"],
        valid=l, rows=_tile(lp, 256), x_blk=4 * d_a // d_b, y_blk=4 * d_a // d_b + 1)
    if lp != l:
        delta_out, lru_out = delta_out[:, :l], lru_out[:, :l]

    x1 = _out_proj(xf, delta_out.reshape(t, d_a), lru_out.reshape(t, d_b), p["w_out"], layer,
                   _tile(t, 1024), _tile(d, 1024))
    f = p["w_mlp_up"].shape[2]
    hid = _mlp_up(x1, p["w_norm_mlp"], p["w_mlp_up"], layer, tm, _tile(f, 1024))
    x2 = _mlp_down(x1, hid, p["w_mlp_down"], layer, _tile(t, 1024), _tile(d, 1024), _tile(f, 4096))
    if final_norm:
        x2 = _final_norm(x2, w_final, tm)
    return x2.reshape(b, l, d), s_d_new, conv_d_new, h_new[:, 0, :], conv_l_new


def _trunk(x, s_d, conv_d, h_l, conv_l, layers, w_final):
    sds, cds, hls, cls = [], [], [], []
    depth = len(layers)
    for i, p in enumerate(layers):
        x, sd, cd, hl, cl = _layer(x, conv_d[i], s_d[i], conv_l[i], h_l[i], p, w_final, i == depth - 1)
        sds.append(sd)
        cds.append(cd)
        hls.append(hl)
        cls.append(cl)
    return x, jnp.stack(sds), jnp.stack(cds), jnp.stack(hls), jnp.stack(cls)


def kernel(x_prompt, x_sample, state_delta, state_conv_delta, state_lru, state_conv_lru,
           w_norm_mix, w_in, w_conv_delta, a_log, dt_bias, w_norm_delta,
           w_conv_lru, b_conv_lru, w_gate_a, b_gate_a, w_gate_x, b_gate_x,
           lam, w_norm_lru, w_out, w_norm_mlp, w_mlp_up, w_mlp_down, w_norm_final):
    params = dict(w_norm_mix=w_norm_mix, w_in=w_in, w_conv_delta=w_conv_delta, a_log=a_log, dt_bias=dt_bias,
                  w_norm_delta=w_norm_delta, w_conv_lru=w_conv_lru, b_conv_lru=b_conv_lru,
                  w_gate_a=w_gate_a, b_gate_a=b_gate_a, w_gate_x=w_gate_x, b_gate_x=b_gate_x,
                  lam=lam, w_norm_lru=w_norm_lru, w_out=w_out, w_norm_mlp=w_norm_mlp,
                  w_mlp_up=w_mlp_up, w_mlp_down=w_mlp_down)
    depth = w_in.shape[0]
    stacked = _prep_stacked(params)
    layers = [_prep_layer(i, params, stacked) for i in range(depth)]
    w_final = w_norm_final[None, :]

    bp = x_prompt.shape[0]
    z_sd = jnp.zeros((depth, bp) + state_delta.shape[2:], F32)
    z_cd = jnp.zeros((depth, bp) + state_conv_delta.shape[2:], x_prompt.dtype)
    z_hl = jnp.zeros((depth, bp) + state_lru.shape[2:], F32)
    z_cl = jnp.zeros((depth, bp) + state_conv_lru.shape[2:], x_prompt.dtype)
    outs_p = _trunk(x_prompt, z_sd, z_cd, z_hl, z_cl, layers, w_final)
    outs_s = _trunk(x_sample, state_delta, state_conv_delta, state_lru, state_conv_lru, layers, w_final)
    return (outs_p[0], outs_s[0]) + outs_p[1:] + outs_s[1:]
```

```python
import functools

import jax
import jax.numpy as jnp
from jax import lax
from jax.experimental import pallas as pl
from jax.experimental.pallas import tpu as pltpu

HEAD_DIM = 128
LRU_BLOCK = 128
CONV_W = 4
LRU_C = 8.0
EPS = 1e-6
LANES = 128
SUBLANES = 8
CHUNK = 128
NILPOTENT_BLOCK = 16
GROUP_CHUNKS = 4
DELTA_UNITS = 16
SQRT_FLOOR = 1e-30
VMEM_LIMIT = 60 * 1024 * 1024

F32 = jnp.float32
BF16 = jnp.bfloat16


def _cparams(sem):
    return pltpu.CompilerParams(dimension_semantics=sem, vmem_limit_bytes=VMEM_LIMIT)


def _dot(a, b):
    return jnp.dot(a, b, preferred_element_type=F32)


def _dot_nt(a, b):
    return lax.dot_general(a, b, (((1,), (1,)), ((), ())), preferred_element_type=F32)


def _split_bf16(x):
    hi = x.astype(BF16)
    lo = (x - hi.astype(F32)).astype(BF16)
    return hi, lo


def _dot3(xs, ys):
    (xh, xl), (yh, yl) = xs, ys
    lhs = jnp.concatenate([xh, xl, xh], axis=1)
    rhs = jnp.concatenate([yh, yh, yl], axis=0)
    return _dot(lhs, rhs)


def _cumsum_rows(tril16, x):
    hi, lo = _split_bf16(x)
    lo2 = (x - hi.astype(F32) - lo.astype(F32)).astype(BF16)
    lhs = jnp.concatenate([tril16, tril16, tril16], axis=1)
    rhs = jnp.concatenate([hi, lo, lo2], axis=0)
    return _dot(lhs, rhs)


def _sigmoid(x):
    return 0.5 * jnp.tanh(0.5 * x) + 0.5


def _softplus(x):
    return jnp.maximum(x, 0.0) + jnp.log1p(jnp.exp(-jnp.abs(x)))


def _silu(x):
    return x * _sigmoid(x)


def _gelu_tanh(x):
    c = 0.7978845608028654
    return 0.5 * x * (1.0 + jnp.tanh(c * (x + 0.044715 * (x * x * x))))


def _rms_rows(x, w):
    ms = jnp.mean(x * x, axis=-1, keepdims=True)
    return (x * lax.rsqrt(ms + EPS)) * w


def _norm_proj_kernel(x_ref, wn_ref, w_ref, wg_ref, arow_ref, dtrow_ref, o_ref, g_ref, xn_ref, *, heads):
    @pl.when(pl.program_id(1) == 0)
    def _():
        xn_ref[...] = _rms_rows(x_ref[...], wn_ref[...]).astype(BF16)
        logits = _dot(xn_ref[...], wg_ref[...])
        lane = lax.broadcasted_iota(jnp.int32, logits.shape, 1)
        log_decay = -jnp.exp(arow_ref[...]) * _softplus(logits + dtrow_ref[...])
        g_ref[...] = jnp.where(lane < heads, _sigmoid(logits), log_decay)

    o_ref[...] = _dot(xn_ref[...], w_ref[...])


def _norm_proj(x, wn, w_main, w_gate, a_row, dt_row, layer, heads, tm, tn):
    t, d = x.shape
    n = w_main.shape[2]
    row_spec = pl.BlockSpec((1, LANES), lambda i, j: (0, 0))
    return pl.pallas_call(
        functools.partial(_norm_proj_kernel, heads=heads),
        grid=(t // tm, n // tn),
        in_specs=[
            pl.BlockSpec((tm, d), lambda i, j: (i, 0)),
            pl.BlockSpec((1, d), lambda i, j: (0, 0)),
            pl.BlockSpec((None, d, tn), lambda i, j: (layer, 0, j)),
            pl.BlockSpec((None, d, LANES), lambda i, j: (layer, 0, 0)),
            row_spec, row_spec,
        ],
        out_specs=[
            pl.BlockSpec((tm, tn), lambda i, j: (i, j)),
            pl.BlockSpec((tm, LANES), lambda i, j: (i, 0)),
        ],
        out_shape=[jax.ShapeDtypeStruct((t, n), F32), jax.ShapeDtypeStruct((t, LANES), F32)],
        scratch_shapes=[pltpu.VMEM((tm, d), BF16)],
        compiler_params=_cparams(("arbitrary", "arbitrary")),
        name="norm_proj",
    )(x, wn, w_main, w_gate, a_row, dt_row)


def _conv_block(ext_ref, x, prev_ref, w_ref, first, rows):
    lo = SUBLANES - (CONV_W - 1)

    @pl.when(first)
    def _():
        ext_ref[lo:SUBLANES, :] = prev_ref[0]

    @pl.when(jnp.logical_not(first))
    def _():
        ext_ref[lo:SUBLANES, :] = ext_ref[rows + lo:rows + SUBLANES, :]

    ext_ref[SUBLANES:SUBLANES + rows, :] = x
    y = ext_ref[lo:lo + rows, :] * w_ref[0:1, :]
    for j in range(1, CONV_W):
        y = y + ext_ref[lo + j:lo + j + rows, :] * w_ref[j:j + 1, :]
    return y


def _unit_lower_inverse_stages(mats, row, col, live, out):
    n = min(mats[0].shape[0], live)
    eye = (row == col).astype(F32)
    shift = NILPOTENT_BLOCK.bit_length() - 1
    same = (row >> shift) == (col >> shift)
    ps = [jnp.where(same, a, 0.0) for a in mats]
    ts = [eye - p for p in ps]
    pss = [_split_bf16(p) for p in ps]
    size = 2
    while size < NILPOTENT_BLOCK:
        pss = [_split_bf16(_dot3(p, p)) for p in pss]
        yield
        ts = [t + _dot3(_split_bf16(t), p) for t, p in zip(ts, pss)]
        yield
        size *= 2
    blk = NILPOTENT_BLOCK
    tss = [_split_bf16(t) for t in ts]
    while blk < n:
        shift += 1
        nxt = (row >> shift) == (col >> shift)
        sel = jnp.logical_and(nxt, jnp.logical_not(same))
        nss = [_split_bf16(_dot3(t, _split_bf16(jnp.where(sel, a, 0.0)))) for t, a in zip(tss, mats)]
        yield
        ts = [t - _dot3(nm, t2) for t, nm, t2 in zip(ts, nss, tss)]
        tss = [_split_bf16(t) for t in ts]
        yield
        same = nxt
        blk *= 2
    out.extend(tss)


def _interleave(main, side, every):
    for count, _ in enumerate(main, 1):
        if count % every == 0:
            next(side, None)
    for _ in side:
        pass


def _delta_kernel(q_ref, k_ref, v_ref, z_ref, g_ref, pq_ref, pk_ref, pv_ref, s0_ref,
                  wq_ref, wk_ref, wv_ref, wnorm_ref,
                  o_ref, s_ref, eq_ref, ek_ref, ev_ref, st_ref, *, rows, nblk, valid, heads, hpb):
    head0 = pl.program_id(1) * hpb
    l = pl.program_id(2)
    first = l == 0

    @pl.when(first)
    def _():
        st_ref[...] = s0_ref[0]

    q_all = _silu(_conv_block(eq_ref, q_ref[0], pq_ref, wq_ref, first, rows))
    k_all = _silu(_conv_block(ek_ref, k_ref[0], pk_ref, wk_ref, first, rows))
    v_all = _silu(_conv_block(ev_ref, v_ref[0], pv_ref, wv_ref, first, rows))
    zg = _silu(z_ref[0])

    gates = g_ref[...]
    lane = lax.broadcasted_iota(jnp.int32, gates.shape, 1)
    if valid < rows * nblk:
        live_rows = l * rows + lax.broadcasted_iota(jnp.int32, (rows, 1), 0) < valid
    per_head = []
    for j in range(hpb):
        hs = slice(j * HEAD_DIM, (j + 1) * HEAD_DIM)
        q, k = q_all[:, hs], k_all[:, hs]
        q = q * lax.rsqrt(jnp.sum(q * q, axis=-1, keepdims=True) + EPS) * (HEAD_DIM ** -0.5)
        k = k * lax.rsqrt(jnp.sum(k * k, axis=-1, keepdims=True) + EPS)
        beta = jnp.sum(jnp.where(lane == head0 + j, gates, 0.0), axis=1, keepdims=True)
        g = jnp.sum(jnp.where(lane == heads + head0 + j, gates, 0.0), axis=1, keepdims=True)
        if valid < rows * nblk:
            beta = jnp.where(live_rows, beta, 0.0)
            g = jnp.where(live_rows, g, 0.0)
            k = jnp.where(live_rows, k, 0.0)
        per_head.append((q, k, v_all[:, hs], beta, g))

    row = lax.broadcasted_iota(jnp.int32, (CHUNK, CHUNK), 0)
    col = lax.broadcasted_iota(jnp.int32, (CHUNK, CHUNK), 1)
    causal = row >= col
    strict = row > col
    tril16 = causal.astype(BF16)

    live = min(CHUNK, valid) if nblk * rows == CHUNK else CHUNK

    def prepare(units, out):
        pre = []
        for j, c in units:
            q, k, v, beta, g = per_head[j]
            sl = slice(c * CHUNK, (c + 1) * CHUNK)
            qc, kc, vc = q[sl], k[sl], v[sl]
            beta_b = jnp.broadcast_to(beta[sl], (CHUNK, CHUNK))
            g_b = jnp.broadcast_to(g[sl], (CHUNK, CHUNK))
            gcum = _cumsum_rows(tril16, g_b)
            decay = jnp.where(causal, jnp.exp(jnp.where(causal, gcum - gcum.T, 0.0)), 0.0)
            eg = jnp.exp(gcum)
            kb = kc * beta_b
            kc16 = kc.astype(BF16)
            a_mat = jnp.where(strict, _dot_nt(kb.astype(BF16), kc16) * decay, 0.0)
            g_last = gcum[CHUNK - 1:CHUNK, :]
            pre.append(dict(
                a=a_mat, rhs=jnp.concatenate([vc * beta_b, kb * eg], axis=1),
                qk=(_dot_nt(qc.astype(BF16), kc16) * decay).astype(BF16), qg=qc * eg,
                kdt=(kc * jnp.exp(g_last - gcum)).T.astype(BF16), gl=jnp.exp(g_last)))
            yield
        t_invs = []
        yield from _unit_lower_inverse_stages([p["a"] for p in pre], row, col, live, t_invs)
        sols = [_dot3(t, _split_bf16(p["rhs"])).astype(BF16) for t, p in zip(t_invs, pre)]
        yield
        for unit, p, sol in zip(units, pre, sols):
            ks = _dot(p["kdt"], sol)
            qs = _dot(p["qk"], sol)
            lhs = jnp.concatenate([p["qg"] - qs[:, HEAD_DIM:], ks[:, HEAD_DIM:]], axis=0).astype(BF16)
            out[unit] = (lhs, qs[:, :HEAD_DIM], ks[:, :HEAD_DIM], p["gl"])
        yield

    state = [st_ref[j] for j in range(hpb)]

    def recur(units, ops):
        for j, c in units:
            sl = slice(c * CHUNK, (c + 1) * CHUNK)
            hs = slice(j * HEAD_DIM, (j + 1) * HEAD_DIM)
            lhs, o0, s_add, gl = ops[(j, c)]
            s = state[j]
            both = _dot(lhs, s.astype(BF16))
            o = both[:CHUNK] + o0
            state[j] = (s * gl - both[CHUNK:]) + s_add
            o_ref[0, sl, hs] = (_rms_rows(o, wnorm_ref[...]) * zg[sl, hs]).astype(o_ref.dtype)
            yield

    units = [(j, c) for j in range(hpb) for c in range(rows // CHUNK)]
    n = len(units)
    sizes = [n] if n <= GROUP_CHUNKS else [n // 2] + [GROUP_CHUNKS] * (n // 2 // GROUP_CHUNKS)
    n_products = 2 * (NILPOTENT_BLOCK.bit_length() - 2) + 2 * ((CHUNK // NILPOTENT_BLOCK).bit_length() - 1)
    ops = {}
    pending, n_pending, start = iter(()), 1, 0
    for size in sizes:
        group = units[start:start + size]
        start += size
        every = max(1, (size + n_products + 2) // n_pending)
        _interleave(prepare(group, ops), pending, every)
        pending, n_pending = recur(group, ops), size
    for _ in pending:
        pass
    for j in range(hpb):
        st_ref[j] = state[j]

    @pl.when(l == nblk - 1)
    def _():
        s_ref[0] = st_ref[...]


def _delta_mixer(proj, gates, conv_prev, s0, w_conv, w_norm, *, valid, rows, hpb):
    b, l, _ = proj.shape
    heads = s0.shape[1]
    nl = l // rows
    ng = heads // hpb
    width = hpb * HEAD_DIM
    blk = lambda off: pl.BlockSpec((1, rows, width), lambda bi, gi, li, off=off: (bi, li, off * ng + gi))
    prev = lambda off: pl.BlockSpec((1, CONV_W - 1, width), lambda bi, gi, li, off=off: (bi, 0, off * ng + gi))
    wcv = lambda off: pl.BlockSpec((CONV_W, width), lambda bi, gi, li, off=off: (0, off * ng + gi))
    row_spec = pl.BlockSpec((1, LANES), lambda bi, gi, li: (0, 0))
    state_spec = pl.BlockSpec((1, hpb, HEAD_DIM, HEAD_DIM), lambda bi, gi, li: (bi, gi, 0, 0))
    kern = functools.partial(_delta_kernel, rows=rows, nblk=nl, valid=valid, heads=heads, hpb=hpb)
    return pl.pallas_call(
        kern,
        grid=(b, ng, nl),
        in_specs=[blk(0), blk(1), blk(2), blk(3),
                  pl.BlockSpec((rows, LANES), lambda bi, gi, li: (bi * nl + li, 0)),
                  prev(0), prev(1), prev(2), state_spec,
                  wcv(0), wcv(1), wcv(2), row_spec],
        out_specs=[pl.BlockSpec((1, rows, width), lambda bi, gi, li: (bi, li, gi)), state_spec],
        out_shape=[jax.ShapeDtypeStruct((b, l, heads * HEAD_DIM), BF16),
                   jax.ShapeDtypeStruct(s0.shape, F32)],
        scratch_shapes=[pltpu.VMEM((rows + SUBLANES, width), F32)] * 3
        + [pltpu.VMEM((hpb, HEAD_DIM, HEAD_DIM), F32)],
        compiler_params=_cparams(("arbitrary", "arbitrary", "arbitrary")),
        name="delta_mixer",
    )(proj, proj, proj, proj, gates, conv_prev, conv_prev, conv_prev, s0,
      w_conv, w_conv, w_conv, w_norm)


def _lru_kernel(x_ref, y_ref, prev_ref, h0_ref, wc_ref, bc_ref, wa_ref, ba_ref, wx_ref, bx_ref,
                lam_ref, wn_ref, o_ref, hl_ref, ext_ref, a_ref, b_ref, h_ref, *, rows, valid, groups):
    l = pl.program_id(1)
    first = l == 0

    @pl.when(first)
    def _():
        h_ref[...] = h0_ref[0]

    xc = _conv_block(ext_ref, x_ref[0], prev_ref, wc_ref, first, rows) + bc_ref[...]
    sub = lax.broadcasted_iota(jnp.int32, (rows // SUBLANES, SUBLANES, LRU_BLOCK), 1)
    for gi in range(groups):
        cs = slice(gi * LRU_BLOCK, (gi + 1) * LRU_BLOCK)
        xg = xc[:, cs]
        xg16 = xg.astype(BF16)
        gate_r = _sigmoid(_dot(xg16, wa_ref[gi]) + ba_ref[:, cs])
        gate_i = _sigmoid(_dot(xg16, wx_ref[gi]) + bx_ref[:, cs])
        log_a = -LRU_C * gate_r * _softplus(-lam_ref[:, cs])
        a = jnp.exp(log_a)
        var = 1.0 - jnp.exp(2.0 * log_a)
        bb = (var * lax.rsqrt(jnp.maximum(var, SQRT_FLOOR))) * gate_i * xg
        a = a.reshape(rows // SUBLANES, SUBLANES, LRU_BLOCK)
        bb = bb.reshape(rows // SUBLANES, SUBLANES, LRU_BLOCK)
        for s in (1, 2, 4):
            keep = sub >= s
            a_sh = jnp.where(keep, pltpu.roll(a, s, 1), 1.0)
            b_sh = jnp.where(keep, pltpu.roll(bb, s, 1), 0.0)
            bb = a * b_sh + bb
            a = a * a_sh
        a_ref[:, cs] = a.reshape(rows, LRU_BLOCK)
        b_ref[:, cs] = bb.reshape(rows, LRU_BLOCK)

    def step(i, h_prev):
        r0 = pl.multiple_of(i * SUBLANES, SUBLANES)
        hb = a_ref[pl.ds(r0, SUBLANES), :] * h_prev + b_ref[pl.ds(r0, SUBLANES), :]
        b_ref[pl.ds(r0, SUBLANES), :] = hb
        return hb[SUBLANES - 1:SUBLANES, :]

    h_ref[...] = lax.fori_loop(0, rows // SUBLANES, step, h_ref[...])

    hs = b_ref[...]
    last_blk, last_row = divmod(valid - 1, rows)

    @pl.when(l == last_blk)
    def _():
        hl_ref[0] = hs[last_row:last_row + 1, :]

    y = _gelu_tanh(y_ref[0]) * hs
    o_ref[0] = _rms_rows(y, wn_ref[...]).astype(o_ref.dtype)


def _lru_mixer(proj, conv_prev, h0, w_conv, b_conv, w_ga, b_ga, w_gx, b_gx, lam, w_norm,
               *, valid, rows, x_blk, y_blk):
    b, l, _ = proj.shape
    d = h0.shape[-1]
    groups = d // LRU_BLOCK
    nl = l // rows
    vec = pl.BlockSpec((1, d), lambda bi, li: (0, 0))
    gw = pl.BlockSpec((groups, LRU_BLOCK, LRU_BLOCK), lambda bi, li: (0, 0, 0))
    kern = functools.partial(_lru_kernel, rows=rows, valid=valid, groups=groups)
    return pl.pallas_call(
        kern,
        grid=(b, nl),
        in_specs=[pl.BlockSpec((1, rows, d), lambda bi, li: (bi, li, x_blk)),
                  pl.BlockSpec((1, rows, d), lambda bi, li: (bi, li, y_blk)),
                  pl.BlockSpec((1, CONV_W - 1, d), lambda bi, li: (bi, 0, 0)),
                  pl.BlockSpec((1, 1, d), lambda bi, li: (bi, 0, 0)),
                  pl.BlockSpec((CONV_W, d), lambda bi, li: (0, 0)),
                  vec, gw, vec, gw, vec, vec, vec],
        out_specs=[pl.BlockSpec((1, rows, d), lambda bi, li: (bi, li, 0)),
                   pl.BlockSpec((1, 1, d), lambda bi, li: (bi, 0, 0))],
        out_shape=[jax.ShapeDtypeStruct((b, l, d), BF16), jax.ShapeDtypeStruct((b, 1, d), F32)],
        scratch_shapes=[pltpu.VMEM((rows + SUBLANES, d), F32), pltpu.VMEM((rows, d), F32),
                        pltpu.VMEM((rows, d), F32), pltpu.VMEM((1, d), F32)],
        compiler_params=_cparams(("arbitrary", "arbitrary")),
        name="lru_mixer",
    )(proj, proj, conv_prev, h0, w_conv, b_conv, w_ga, b_ga, w_gx, b_gx, lam, w_norm)


def _out_proj_kernel(x_ref, a_ref, b_ref, wa_ref, wb_ref, o_ref):
    o_ref[...] = x_ref[...] + _dot(a_ref[...], wa_ref[...]) + _dot(b_ref[...], wb_ref[...])


def _out_proj(x, mix_a, mix_b, w_out, layer, tm, tn):
    t, d = x.shape
    da, db = mix_a.shape[1], mix_b.shape[1]
    assert da == db
    return pl.pallas_call(
        _out_proj_kernel,
        grid=(t // tm, d // tn),
        in_specs=[pl.BlockSpec((tm, tn), lambda i, j: (i, j)),
                  pl.BlockSpec((tm, da), lambda i, j: (i, 0)),
                  pl.BlockSpec((tm, db), lambda i, j: (i, 0)),
                  pl.BlockSpec((None, da, tn), lambda i, j: (layer, 0, j)),
                  pl.BlockSpec((None, db, tn), lambda i, j: (layer, 1, j))],
        out_specs=pl.BlockSpec((tm, tn), lambda i, j: (i, j)),
        out_shape=jax.ShapeDtypeStruct((t, d), F32),
        compiler_params=_cparams(("arbitrary", "arbitrary")),
        name="out_proj",
    )(x, mix_a, mix_b, w_out, w_out)


def _mlp_up_kernel(x_ref, wn_ref, wu_ref, h_ref, xn_ref):
    @pl.when(pl.program_id(1) == 0)
    def _():
        xn_ref[...] = _rms_rows(x_ref[...], wn_ref[...]).astype(BF16)

    hid = jnp.maximum(_dot(xn_ref[...], wu_ref[...]), 0.0)
    h_ref[...] = (hid * hid).astype(BF16)


def _mlp_up(x, wn, w_up, layer, tm, tn):
    t, d = x.shape
    f = w_up.shape[2]
    return pl.pallas_call(
        _mlp_up_kernel,
        grid=(t // tm, f // tn),
        in_specs=[pl.BlockSpec((tm, d), lambda i, j: (i, 0)),
                  pl.BlockSpec((1, d), lambda i, j: (0, 0)),
                  pl.BlockSpec((None, d, tn), lambda i, j: (layer, 0, j))],
        out_specs=pl.BlockSpec((tm, tn), lambda i, j: (i, j)),
        out_shape=jax.ShapeDtypeStruct((t, f), BF16),
        scratch_shapes=[pltpu.VMEM((tm, d), BF16)],
        compiler_params=_cparams(("arbitrary", "arbitrary")),
        name="mlp_up",
    )(x, wn, w_up)


def _mlp_down_kernel(x_ref, h_ref, wd_ref, o_ref):
    @pl.when(pl.program_id(2) == 0)
    def _():
        o_ref[...] = x_ref[...]

    o_ref[...] += _dot(h_ref[...], wd_ref[...])


def _mlp_down(x, hid, w_down, layer, tm, tn, tk):
    t, d = x.shape
    f = hid.shape[1]
    return pl.pallas_call(
        _mlp_down_kernel,
        grid=(t // tm, d // tn, f // tk),
        in_specs=[pl.BlockSpec((tm, tn), lambda i, j, k: (i, j)),
                  pl.BlockSpec((tm, tk), lambda i, j, k: (i, k)),
                  pl.BlockSpec((None, tk, tn), lambda i, j, k: (layer, k, j))],
        out_specs=pl.BlockSpec((tm, tn), lambda i, j, k: (i, j)),
        out_shape=jax.ShapeDtypeStruct((t, d), F32),
        compiler_params=_cparams(("arbitrary", "arbitrary", "arbitrary")),
        name="mlp_down",
    )(x, hid, w_down)


def _final_norm_kernel(x_ref, w_ref, o_ref):
    o_ref[...] = _rms_rows(x_ref[...], w_ref[...])


def _final_norm(x, w, tm):
    t, d = x.shape
    return pl.pallas_call(
        _final_norm_kernel,
        grid=(t // tm,),
        in_specs=[pl.BlockSpec((tm, d), lambda i: (i, 0)), pl.BlockSpec((1, d), lambda i: (0, 0))],
        out_specs=pl.BlockSpec((tm, d), lambda i: (i, 0)),
        out_shape=jax.ShapeDtypeStruct((t, d), F32),
        compiler_params=_cparams(("arbitrary",)),
        name="final_norm",
    )(x, w)


def _tile(n, pref):
    t = min(n, pref)
    while n % t:
        t //= 2
    return t


def _prep_stacked(p):
    d_a = p["w_conv_delta"].shape[-1] // 3
    heads = p["a_log"].shape[-1]
    w_in = p["w_in"]
    gate_lo, gate_hi = 4 * d_a, 4 * d_a + 2 * heads
    assert gate_lo % LANES == 0 and gate_lo + LANES <= w_in.shape[2] and 2 * heads <= LANES
    return dict(
        w_main=jnp.concatenate([w_in[:, :, :gate_lo], w_in[:, :, gate_hi:]], axis=2).astype(BF16),
        w_gate=w_in[:, :, gate_lo:gate_lo + LANES].astype(BF16),
        w_out=p["w_out"].astype(BF16), w_mlp_up=p["w_mlp_up"].astype(BF16),
        w_mlp_down=p["w_mlp_down"].astype(BF16))


def _prep_layer(i, p, stacked):
    heads = p["a_log"].shape[-1]
    row = lambda v: jnp.pad(v, (heads, LANES - 2 * heads))[None, :].astype(F32)
    return dict(
        stacked, layer=i,
        w_norm_mix=p["w_norm_mix"][i][None, :],
        w_conv_delta=p["w_conv_delta"][i], a_row=row(p["a_log"][i]), dt_row=row(p["dt_bias"][i]),
        w_norm_delta=p["w_norm_delta"][i][None, :],
        w_conv_lru=p["w_conv_lru"][i], b_conv_lru=p["b_conv_lru"][i][None, :],
        w_gate_a=p["w_gate_a"][i].astype(BF16), b_gate_a=p["b_gate_a"][i][None, :],
        w_gate_x=p["w_gate_x"][i].astype(BF16), b_gate_x=p["b_gate_x"][i][None, :],
        lam=p["lam"][i][None, :], w_norm_lru=p["w_norm_lru"][i][None, :],
        w_norm_mlp=p["w_norm_mlp"][i][None, :],
    )


def _layer(x, conv_d, s_d, conv_l, h_l, p, w_final, final_norm):
    b, l, d = x.shape
    t = b * l
    d_a = p["w_conv_delta"].shape[-1] // 3
    d_b = p["w_conv_lru"].shape[-1]
    assert d_a == d_b and d_a % HEAD_DIM == 0
    layer = p["layer"]
    tm = _tile(t, 512)
    xf = x.reshape(t, d)

    proj, gates = _norm_proj(xf, p["w_norm_mix"], p["w_main"], p["w_gate"], p["a_row"], p["dt_row"], layer,
                             d_a // HEAD_DIM, tm, _tile(p["w_main"].shape[2], 1536))
    n_main = proj.shape[1]
    proj = proj.reshape(b, l, n_main)
    conv_d_new = jnp.concatenate([conv_d, proj[:, :, :3 * d_a]], axis=1)[:, -(CONV_W - 1):]
    conv_l_new = jnp.concatenate([conv_l, proj[:, :, 4 * d_a:4 * d_a + d_b]], axis=1)[:, -(CONV_W - 1):]

    lp = -(-l // CHUNK) * CHUNK
    if lp != l:
        proj = jnp.pad(proj, ((0, 0), (0, lp - l), (0, 0)))
        gates = jnp.pad(gates.reshape(b, l, LANES), ((0, 0), (0, lp - l), (0, 0))).reshape(b * lp, LANES)

    delta_rows = _tile(lp, DELTA_UNITS * CHUNK)
    delta_out, s_d_new = _delta_mixer(
        proj, gates, conv_d, s_d, p["w_conv_delta"], p["w_norm_delta"],
        valid=l, rows=delta_rows, hpb=_tile(d_a // HEAD_DIM, DELTA_UNITS * CHUNK // delta_rows))
    lru_out, h_new = _lru_mixer(
        proj, conv_l, h_l[:, None, :], p["w_conv_lru"], p["b_conv_lru"], p["w_gate_a"], p["b_gate_a"],
        p["w_gate_x"], p["b_gate_x"], p["lam"], p["w_norm_lru"],
        valid=l, rows=_tile(lp, 256), x_blk=4 * d_a // d_b, y_blk=4 * d_a // d_b + 1)
    if lp != l:
        delta_out, lru_out = delta_out[:, :l], lru_out[:, :l]

    x1 = _out_proj(xf, delta_out.reshape(t, d_a), lru_out.reshape(t, d_b), p["w_out"], layer,
                   _tile(t, 1024), _tile(d, 1024))
    f = p["w_mlp_up"].shape[2]
    hid = _mlp_up(x1, p["w_norm_mlp"], p["w_mlp_up"], layer, tm, _tile(f, 1024))
    x2 = _mlp_down(x1, hid, p["w_mlp_down"], layer, _tile(t, 1024), _tile(d, 1024), _tile(f, 4096))
    if final_norm:
        x2 = _final_norm(x2, w_final, tm)
    return x2.reshape(b, l, d), s_d_new, conv_d_new, h_new[:, 0, :], conv_l_new


def _trunk(x, s_d, conv_d, h_l, conv_l, layers, w_final):
    sds, cds, hls, cls = [], [], [], []
    depth = len(layers)
    for i, p in enumerate(layers):
        x, sd, cd, hl, cl = _layer(x, conv_d[i], s_d[i], conv_l[i], h_l[i], p, w_final, i == depth - 1)
        sds.append(sd)
        cds.append(cd)
        hls.append(hl)
        cls.append(cl)
    return x, jnp.stack(sds), jnp.stack(cds), jnp.stack(hls), jnp.stack(cls)


def kernel(x_prompt, x_sample, state_delta, state_conv_delta, state_lru, state_conv_lru,
           w_norm_mix, w_in, w_conv_delta, a_log, dt_bias, w_norm_delta,
           w_conv_lru, b_conv_lru, w_gate_a, b_gate_a, w_gate_x, b_gate_x,
           lam, w_norm_lru, w_out, w_norm_mlp, w_mlp_up, w_mlp_down, w_norm_final):
    params = dict(w_norm_mix=w_norm_mix, w_in=w_in, w_conv_delta=w_conv_delta, a_log=a_log, dt_bias=dt_bias,
                  w_norm_delta=w_norm_delta, w_conv_lru=w_conv_lru, b_conv_lru=b_conv_lru,
                  w_gate_a=w_gate_a, b_gate_a=b_gate_a, w_gate_x=w_gate_x, b_gate_x=b_gate_x,
                  lam=lam, w_norm_lru=w_norm_lru, w_out=w_out, w_norm_mlp=w_norm_mlp,
                  w_mlp_up=w_mlp_up, w_mlp_down=w_mlp_down)
    depth = w_in.shape[0]
    stacked = _prep_stacked(params)
    layers = [_prep_layer(i, params, stacked) for i in range(depth)]
    w_final = w_norm_final[None, :]

    bp = x_prompt.shape[0]
    z_sd = jnp.zeros((depth, bp) + state_delta.shape[2:], F32)
    z_cd = jnp.zeros((depth, bp) + state_conv_delta.shape[2:], x_prompt.dtype)
    z_hl = jnp.zeros((depth, bp) + state_lru.shape[2:], F32)
    z_cl = jnp.zeros((depth, bp) + state_conv_lru.shape[2:], x_prompt.dtype)
    outs_p = _trunk(x_prompt, z_sd, z_cd, z_hl, z_cl, layers, w_final)
    outs_s = _trunk(x_sample, state_delta, state_conv_delta, state_lru, state_conv_lru, layers, w_final)
    return (outs_p[0], outs_s[0]) + outs_p[1:] + outs_s[1:]
```
